```python
import math
import jax
import jax.numpy as jnp
from jax import lax
import numpy as np

D_MODEL = 4096
BATCH = 1
SEQ = 8192
DEPTH = 1
DEC_BATCH = 128
DEC_SEQ = 8
PAST_LEN = 2048
PAGE_SIZE = 128

D_RNN = D_MODEL // 2
RNN_BLOCK = 128
N_RNN_BLOCKS = D_RNN // RNN_BLOCK
CONV_W = 4
LRU_C = 8.0
HEAD_DIM = 128
D_ATTN = D_MODEL - D_RNN
N_HEADS = D_ATTN // HEAD_DIM
N_KV = 4
GQA = N_HEADS // N_KV
N_IDX_HEADS = 16
D_IDX = 128
IDX_SCALE = (N_IDX_HEADS * D_IDX) ** -0.5
TOPK_MAX = 256
QBLOCK = 128
N_BUCKETS = 32
MAX_DISTANCE = 128
D_FF = -(-8 * D_MODEL // (3 * 256)) * 256
D_MIX = D_RNN + D_ATTN
IN_SIZES = (D_RNN, D_RNN, N_HEADS * HEAD_DIM, N_KV * HEAD_DIM, N_KV * HEAD_DIM,
            N_IDX_HEADS * D_IDX, D_IDX, N_IDX_HEADS)
IN_COLS = 2 * D_RNN + (N_HEADS + 2 * N_KV) * HEAD_DIM + N_IDX_HEADS * D_IDX + D_IDX + N_IDX_HEADS
NEG = -1e30
EPS = 1e-6

kernel_name = 'hymba_rglru_dsa_t5_adaln_decode_step'


def rmsnorm(x, g):
    xf = x.astype(jnp.float32)
    y = xf * lax.rsqrt(jnp.mean(xf * xf, axis=-1, keepdims=True) + EPS)
    return (y * g.astype(jnp.float32)).astype(x.dtype)


def layernorm(x, g, b):
    xf = x.astype(jnp.float32)
    mu = jnp.mean(xf, axis=-1, keepdims=True)
    var = jnp.mean(jnp.square(xf - mu), axis=-1, keepdims=True)
    y = (xf - mu) * lax.rsqrt(var + EPS)
    return (y * g.astype(jnp.float32) + b.astype(jnp.float32)).astype(x.dtype)


def t5_bucket(dist):
    max_exact = N_BUCKETS // 2
    d = jnp.maximum(dist, 1).astype(jnp.float32)
    large = max_exact + (jnp.log(d / max_exact) / math.log(MAX_DISTANCE / max_exact)
                         * (N_BUCKETS - max_exact)).astype(jnp.int32)
    large = jnp.minimum(large, N_BUCKETS - 1)
    return jnp.where(dist < max_exact, dist, large)


def modulation(c, w_mod, b_mod):
    m = jax.nn.silu(c) @ w_mod + b_mod
    return jnp.split(m[:, None, :], 6, axis=-1)


def mixer_in(x, c, w_mod, b_mod, g_pre_mix, w_in, idx_k_ln_g, idx_k_ln_b):
    mods = modulation(c, w_mod, b_mod)
    shift1, scale1 = mods[0], mods[1]
    h = rmsnorm(x, g_pre_mix) * (1 + scale1) + shift1
    z = h @ w_in
    points = [int(p) for p in np.cumsum(IN_SIZES)[:-1]]
    xr, gate, q, k, v, qi, ki, wi = jnp.split(z, points, axis=-1)
    B, T = x.shape[:2]
    q = q.reshape(B, T, N_HEADS, HEAD_DIM)
    k = k.reshape(B, T, N_KV, HEAD_DIM)
    v = v.reshape(B, T, N_KV, HEAD_DIM)
    qi = qi.reshape(B, T, N_IDX_HEADS, D_IDX)
    ki = layernorm(ki, idx_k_ln_g, idx_k_ln_b)
    wi = wi * IDX_SCALE
    return mods, xr, gate, q, k, v, qi, ki, wi


def rglru(xr, gate, conv_buf, h0, conv_w, conv_b, w_rg_a, b_rg_a, w_rg_i, b_rg_i, lru_lambda):
    B, T, _ = xr.shape
    xp = jnp.concatenate([conv_buf.astype(xr.dtype), xr], axis=1)
    conv = conv_b + xp[:, 0:T] * conv_w[0]
    for j in range(1, CONV_W):
        conv = conv + xp[:, j:j + T] * conv_w[j]
    new_buf = xp[:, T:]
    xb = conv.reshape(B, T, N_RNN_BLOCKS, RNN_BLOCK)
    r = jax.nn.sigmoid(jnp.einsum('btnd,nde->btne', xb, w_rg_a).reshape(B, T, D_RNN) + b_rg_a)
    i = jax.nn.sigmoid(jnp.einsum('btnd,nde->btne', xb, w_rg_i).reshape(B, T, D_RNN) + b_rg_i)
    log_a = -LRU_C * r.astype(jnp.float32) * jax.nn.softplus(-lru_lambda.astype(jnp.float32))
    a = jnp.exp(log_a)
    u = jnp.sqrt(-jnp.expm1(2.0 * log_a)) * (i * conv).astype(jnp.float32)

    def step(h, au):
        a_t, u_t = au
        h = a_t * h + u_t
        return h, h

    h_last, hs = lax.scan(step, h0.astype(jnp.float32), (jnp.swapaxes(a, 0, 1), jnp.swapaxes(u, 0, 1)))
    y = jnp.swapaxes(hs, 0, 1).astype(xr.dtype) * jax.nn.gelu(gate)
    return y, new_buf, h_last.astype(h0.dtype)


def indexer_topk(qi, wi, kidx, q_pos, n_sel):
    s = jax.nn.relu(jnp.einsum('bthd,bsd->bths', qi, kidx).astype(jnp.float32))
    score = jnp.einsum('bths,bth->bts', s, wi.astype(jnp.float32))
    L = kidx.shape[1]
    visible = jnp.arange(L)[None, None, :] <= q_pos[None, :, None]
    score = jnp.where(visible, score, NEG)
    _, idx = lax.top_k(score, n_sel)
    return idx


def attend(q, k_sel, v_sel, idx, q_pos, rel_bias):
    B, T = q.shape[:2]
    S = idx.shape[-1]
    qg = q.reshape(B, T, N_KV, GQA, HEAD_DIM)
    logits = jnp.einsum('btkgd,btskd->btkgs', qg, k_sel).astype(jnp.float32) * (HEAD_DIM ** -0.5)
    dist = q_pos[None, :, None] - idx
    bias = rel_bias[t5_bucket(jnp.maximum(dist, 0))]
    bias = bias.reshape(B, T, S, N_KV, GQA).transpose(0, 1, 3, 4, 2)
    logits = logits + bias.astype(jnp.float32)
    logits = jnp.where((dist >= 0)[:, :, None, None, :], logits, NEG)
    p = jax.nn.softmax(logits, axis=-1).astype(v_sel.dtype)
    o = jnp.einsum('btkgs,btskd->btkgd', p, v_sel)
    return o.reshape(B, T, N_HEADS * HEAD_DIM)


def prompt_attention(q, k, v, qi, wi, kidx, rel_bias):
    B, T = q.shape[:2]
    n_sel = min(TOPK_MAX, T // 4)
    b_ix = jnp.arange(B)[:, None, None]

    def block(bi):
        s0 = bi * QBLOCK
        qb = lax.dynamic_slice_in_dim(q, s0, QBLOCK, axis=1)
        qib = lax.dynamic_slice_in_dim(qi, s0, QBLOCK, axis=1)
        wib = lax.dynamic_slice_in_dim(wi, s0, QBLOCK, axis=1)
        pos = s0 + jnp.arange(QBLOCK)
        idx = indexer_topk(qib, wib, kidx, pos, n_sel)
        return attend(qb, k[b_ix, idx], v[b_ix, idx], idx, pos, rel_bias)

    out = lax.map(block, jnp.arange(T // QBLOCK))
    return out.transpose(1, 0, 2, 3).reshape(B, T, D_ATTN)


def sample_attention(q, k, v, qi, wi, kidx, cache_k, cache_v, cache_idx_k, page_table, rel_bias):
    DB, T = q.shape[:2]
    past = page_table.shape[1] * PAGE_SIZE
    kidx_past = cache_idx_k[page_table].reshape(DB, past, D_IDX).astype(kidx.dtype)
    kidx_all = jnp.concatenate([kidx_past, kidx], axis=1)
    n_sel = min(TOPK_MAX, (past + T) // 4)
    pos = past + jnp.arange(T)
    idx = indexer_topk(qi, wi, kidx_all, pos, n_sel)
    b_ix = jnp.arange(DB)[:, None, None]
    in_past = (idx < past)[..., None, None]
    pidx = jnp.minimum(idx, past - 1)
    phys = page_table[b_ix, pidx // PAGE_SIZE]
    off = pidx % PAGE_SIZE
    nidx = jnp.clip(idx - past, 0, T - 1)
    k_sel = jnp.where(in_past, cache_k[phys, off].astype(k.dtype), k[b_ix, nidx])
    v_sel = jnp.where(in_past, cache_v[phys, off].astype(v.dtype), v[b_ix, nidx])
    return attend(q, k_sel, v_sel, idx, pos, rel_bias)


def block_out(x, y_rnn, y_attn, mods, g_out_rnn, g_out_attn, w_out, g_post_mix, g_pre_ffn,
              g_post_ffn, w_ffn_gate, w_ffn_up, w_ffn_down):
    gate1, shift2, scale2, gate2 = mods[2], mods[3], mods[4], mods[5]
    mixed = jnp.concatenate([rmsnorm(y_rnn, g_out_rnn), rmsnorm(y_attn, g_out_attn)], axis=-1) @ w_out
    x = x + gate1 * rmsnorm(mixed, g_post_mix)
    h = rmsnorm(x, g_pre_ffn) * (1 + scale2) + shift2
    f = (jax.nn.silu(h @ w_ffn_gate) * (h @ w_ffn_up)) @ w_ffn_down
    return x + gate2 * rmsnorm(f, g_post_ffn)


def setup_inputs(seed: int = 0) -> dict:
    key = jax.random.key(seed)
    keys = iter(jax.random.split(key, 48))

    def nrm(shape, scale):
        return scale * jax.random.normal(next(keys), shape, jnp.float32)

    def gain(shape):
        return 1.0 + nrm(shape, 0.05)

    n_pages = PAST_LEN // PAGE_SIZE
    n_phys = (DEC_BATCH * n_pages * 5) // 4
    page_table = jax.random.permutation(next(keys), n_phys)[: DEC_BATCH * n_pages]
    page_table = page_table.reshape(DEC_BATCH, n_pages).astype(jnp.int32)
    u = jax.random.uniform(next(keys), (DEPTH, D_RNN), jnp.float32, 0.9, 0.999)
    a_base = u ** (1.0 / LRU_C)
    lru_lambda = jnp.log(a_base) - jnp.log1p(-a_base)
    L = DEPTH
    return {
        'x_prompt': nrm((BATCH, SEQ, D_MODEL), 1.0),
        'x_sample': nrm((DEC_BATCH, DEC_SEQ, D_MODEL), 1.0),
        'cache_k': nrm((L, n_phys, PAGE_SIZE, N_KV, HEAD_DIM), 1.0),
        'cache_v': nrm((L, n_phys, PAGE_SIZE, N_KV, HEAD_DIM), 1.0),
        'cache_idx_k': nrm((L, n_phys, PAGE_SIZE, D_IDX), 1.0),
        'state_conv': nrm((L, DEC_BATCH, CONV_W - 1, D_RNN), 1.0),
        'state_h': nrm((L, DEC_BATCH, D_RNN), 0.5),
        'page_table': page_table,
        'c_prompt': nrm((BATCH, D_MODEL), 1.0),
        'c_sample': nrm((DEC_BATCH, D_MODEL), 1.0),
        'rel_bias': nrm((N_BUCKETS, N_HEADS), 0.5),
        'w_mod': nrm((L, D_MODEL, 6 * D_MODEL), 0.5 * D_MODEL ** -0.5),
        'b_mod': nrm((L, 6 * D_MODEL), 0.02),
        'g_pre_mix': gain((L, D_MODEL)),
        'g_post_mix': gain((L, D_MODEL)),
        'g_pre_ffn': gain((L, D_MODEL)),
        'g_post_ffn': gain((L, D_MODEL)),
        'w_in': nrm((L, D_MODEL, IN_COLS), D_MODEL ** -0.5),
        'conv_w': nrm((L, CONV_W, D_RNN), 0.5),
        'conv_b': nrm((L, D_RNN), 0.02),
        'w_rg_a': nrm((L, N_RNN_BLOCKS, RNN_BLOCK, RNN_BLOCK), RNN_BLOCK ** -0.5),
        'b_rg_a': nrm((L, D_RNN), 0.02),
        'w_rg_i': nrm((L, N_RNN_BLOCKS, RNN_BLOCK, RNN_BLOCK), RNN_BLOCK ** -0.5),
        'b_rg_i': nrm((L, D_RNN), 0.02),
        'lru_lambda': lru_lambda,
        'idx_k_ln_g': gain((L, D_IDX)),
        'idx_k_ln_b': nrm((L, D_IDX), 0.02),
        'g_out_rnn': gain((L, D_RNN)),
        'g_out_attn': gain((L, D_ATTN)),
        'w_out': nrm((L, D_MIX, D_MODEL), D_MIX ** -0.5),
        'w_ffn_gate': nrm((L, D_MODEL, D_FF), D_MODEL ** -0.5),
        'w_ffn_up': nrm((L, D_MODEL, D_FF), D_MODEL ** -0.5),
        'w_ffn_down': nrm((L, D_FF, D_MODEL), D_FF ** -0.5),
    }


def reference(x_prompt, x_sample, cache_k, cache_v, cache_idx_k, state_conv, state_h, page_table,
              c_prompt, c_sample, rel_bias, w_mod, b_mod, g_pre_mix, g_post_mix, g_pre_ffn, g_post_ffn,
              w_in, conv_w, conv_b, w_rg_a, b_rg_a, w_rg_i, b_rg_i, lru_lambda, idx_k_ln_g, idx_k_ln_b,
              g_out_rnn, g_out_attn, w_out, w_ffn_gate, w_ffn_up, w_ffn_down):
    y_p, y_s = x_prompt, x_sample
    kp, vp, ip, cvp, hp, ksl, vsl, isl, cvs, hsl = ([] for _ in range(10))
    for l in range(DEPTH):
        in_w = (w_mod[l], b_mod[l], g_pre_mix[l], w_in[l], idx_k_ln_g[l], idx_k_ln_b[l])
        rnn_w = (conv_w[l], conv_b[l], w_rg_a[l], b_rg_a[l], w_rg_i[l], b_rg_i[l], lru_lambda[l])
        out_w = (g_out_rnn[l], g_out_attn[l], w_out[l], g_post_mix[l], g_pre_ffn[l], g_post_ffn[l],
                 w_ffn_gate[l], w_ffn_up[l], w_ffn_down[l])
        mods, xr, gt, q, k, v, qi, ki, wi = mixer_in(y_p, c_prompt, *in_w)
        zero_buf = jnp.zeros((y_p.shape[0], CONV_W - 1, D_RNN), y_p.dtype)
        zero_h = jnp.zeros((y_p.shape[0], D_RNN), y_p.dtype)
        r_out, buf_p, h_p = rglru(xr, gt, zero_buf, zero_h, *rnn_w)
        a_out = prompt_attention(q, k, v, qi, wi, ki, rel_bias)
        y_p = block_out(y_p, r_out, a_out, mods, *out_w)
        kp.append(k); vp.append(v); ip.append(ki); cvp.append(buf_p); hp.append(h_p)
        mods, xr, gt, q, k, v, qi, ki, wi = mixer_in(y_s, c_sample, *in_w)
        r_out, buf_s, h_s = rglru(xr, gt, state_conv[l], state_h[l], *rnn_w)
        a_out = sample_attention(q, k, v, qi, wi, ki, cache_k[l], cache_v[l], cache_idx_k[l],
                                 page_table, rel_bias)
        y_s = block_out(y_s, r_out, a_out, mods, *out_w)
        ksl.append(k); vsl.append(v); isl.append(ki); cvs.append(buf_s); hsl.append(h_s)
    k_prompt = jnp.stack(kp)
    v_prompt = jnp.stack(vp)
    kidx_prompt = jnp.stack(ip)
    conv_prompt = jnp.stack(cvp)
    h_prompt = jnp.stack(hp)
    k_sample = jnp.stack(ksl)
    v_sample = jnp.stack(vsl)
    kidx_sample = jnp.stack(isl)
    conv_sample = jnp.stack(cvs)
    h_sample = jnp.stack(hsl)
    return (y_p, y_s, k_prompt, v_prompt, kidx_prompt, conv_prompt, h_prompt,
            k_sample, v_sample, kidx_sample, conv_sample, h_sample)
```

```python
import functools
import math

import jax
import jax.numpy as jnp
import numpy as np
from jax import lax
from jax.experimental import pallas as pl
from jax.experimental.pallas import tpu as pltpu

HEAD_DIM = 128
N_KV = 4
N_IDX_HEADS = 16
D_IDX = 128
TOPK_MAX = 256
MAX_DISTANCE = 128
LRU_C = 8.0
EPS = 1e-6
NEG = -1e30

LANES = 128
SUBLANES = 8
BF16_ROWS = 16
VMEM_LIMIT_BYTES = 56 * 1024 * 1024
ROWWISE_TILE = 128

F32 = jnp.float32
BF16 = jnp.bfloat16
I32 = jnp.int32
INT_MIN = -(2 ** 31)
NT_DIMS = (((1,), (1,)), ((), ()))


def _sortable_key_of(x):
    b = int(np.array(x, np.float32).view(np.int32))
    return b ^ ((b >> 31) & 0x7FFFFFFF)


NEG_KEY = _sortable_key_of(NEG)


def _params(*semantics):
    return pltpu.CompilerParams(dimension_semantics=semantics, vmem_limit_bytes=VMEM_LIMIT_BYTES)


def _round_up(x, m):
    return -(-x // m) * m


def _row_tile(m, cap):
    t = min(m, cap)
    while m % t:
        t //= 2
    return t


def _sortable_keys(x):
    bits = pltpu.bitcast(x, I32)
    return bits ^ ((bits >> 31) & 0x7FFFFFFF)


def _rms(x, g):
    return x * lax.rsqrt(jnp.mean(x * x, axis=-1, keepdims=True) + EPS) * g


def _mod_kernel(c_ref, w_ref, b_ref, o_ref, a_s):
    @pl.when(pl.program_id(0) == 0)
    def _():
        c = c_ref[...]
        a_s[...] = (c * jax.nn.sigmoid(c)).astype(BF16)

    acc = jnp.dot(a_s[...], w_ref[...].astype(BF16), preferred_element_type=F32)
    o_ref[...] = acc + b_ref[...]


def modulation(c, w_mod, b_mod):
    b, d = c.shape
    n = w_mod.shape[1]
    tn = 512
    return pl.pallas_call(
        _mod_kernel,
        grid=(n // tn,),
        in_specs=[pl.BlockSpec((b, d), lambda j: (0, 0)),
                  pl.BlockSpec((d, tn), lambda j: (0, j)),
                  pl.BlockSpec((1, tn), lambda j: (0, j))],
        out_specs=pl.BlockSpec((b, tn), lambda j: (0, j)),
        out_shape=jax.ShapeDtypeStruct((b, n), F32),
        scratch_shapes=[pltpu.VMEM((b, d), BF16)],
        compiler_params=_params("arbitrary"),
        name="modulation",
    )(c, w_mod, b_mod.reshape(1, n))


def _mod_spec(mod, tm, d):
    if mod.shape[0] == 1:
        return pl.BlockSpec((1, d), lambda i: (0, 0))
    return pl.BlockSpec((tm, d), lambda i: (i, 0))


def _prenorm_kernel(x_ref, g_ref, sc_ref, sh_ref, o_ref):
    y = _rms(x_ref[...], g_ref[...])
    o_ref[...] = (y * (1.0 + sc_ref[...]) + sh_ref[...]).astype(o_ref.dtype)


def prenorm(x, g, scale, shift):
    m, d = x.shape
    tm = _row_tile(m, ROWWISE_TILE)
    return pl.pallas_call(
        _prenorm_kernel,
        grid=(m // tm,),
        in_specs=[pl.BlockSpec((tm, d), lambda i: (i, 0)),
                  pl.BlockSpec((1, d), lambda i: (0, 0)),
                  _mod_spec(scale, tm, d), _mod_spec(shift, tm, d)],
        out_specs=pl.BlockSpec((tm, d), lambda i: (i, 0)),
        out_shape=jax.ShapeDtypeStruct((m, d), BF16),
        compiler_params=_params("parallel"),
        name="prenorm",
    )(x, g.reshape(1, d), scale, shift)


def _group_norm_kernel(a_ref, b_ref, ga_ref, gb_ref, o_ref):
    da = a_ref.shape[1]
    o_ref[:, :da] = _rms(a_ref[...], ga_ref[...]).astype(o_ref.dtype)
    o_ref[:, da:] = _rms(b_ref[...], gb_ref[...]).astype(o_ref.dtype)


def group_norm_concat(a, b, ga, gb):
    m, da = a.shape
    db = b.shape[1]
    tm = _row_tile(m, ROWWISE_TILE)
    return pl.pallas_call(
        _group_norm_kernel,
        grid=(m // tm,),
        in_specs=[pl.BlockSpec((tm, da), lambda i: (i, 0)),
                  pl.BlockSpec((tm, db), lambda i: (i, 0)),
                  pl.BlockSpec((1, da), lambda i: (0, 0)),
                  pl.BlockSpec((1, db), lambda i: (0, 0))],
        out_specs=pl.BlockSpec((tm, da + db), lambda i: (i, 0)),
        out_shape=jax.ShapeDtypeStruct((m, da + db), BF16),
        compiler_params=_params("parallel"),
        name="group_norm_concat",
    )(a, b, ga.reshape(1, da), gb.reshape(1, db))


def _mid_kernel(x_ref, mix_ref, gpost_ref, gate_ref, gpre_ref, sc_ref, sh_ref, x1_ref, h_ref):
    x1 = x_ref[...] + gate_ref[...] * _rms(mix_ref[...], gpost_ref[...])
    x1_ref[...] = x1
    h_ref[...] = (_rms(x1, gpre_ref[...]) * (1.0 + sc_ref[...]) + sh_ref[...]).astype(h_ref.dtype)


def mid_block(x, mixed, g_post, gate1, g_pre, scale2, shift2):
    m, d = x.shape
    tm = _row_tile(m, ROWWISE_TILE)
    row = pl.BlockSpec((tm, d), lambda i: (i, 0))
    vec = pl.BlockSpec((1, d), lambda i: (0, 0))
    return pl.pallas_call(
        _mid_kernel,
        grid=(m // tm,),
        in_specs=[row, row, vec, _mod_spec(gate1, tm, d), vec,
                  _mod_spec(scale2, tm, d), _mod_spec(shift2, tm, d)],
        out_specs=[row, row],
        out_shape=[jax.ShapeDtypeStruct((m, d), F32), jax.ShapeDtypeStruct((m, d), BF16)],
        compiler_params=_params("parallel"),
        name="mid_block",
    )(x, mixed, g_post.reshape(1, d), gate1, g_pre.reshape(1, d), scale2, shift2)


def _final_kernel(x_ref, f_ref, g_ref, gate_ref, o_ref):
    o_ref[...] = x_ref[...] + gate_ref[...] * _rms(f_ref[...], g_ref[...])


def final_block(x1, f, g_post, gate2):
    m, d = x1.shape
    tm = _row_tile(m, ROWWISE_TILE)
    row = pl.BlockSpec((tm, d), lambda i: (i, 0))
    return pl.pallas_call(
        _final_kernel,
        grid=(m // tm,),
        in_specs=[row, row, pl.BlockSpec((1, d), lambda i: (0, 0)), _mod_spec(gate2, tm, d)],
        out_specs=row,
        out_shape=jax.ShapeDtypeStruct((m, d), F32),
        compiler_params=_params("parallel"),
        name="final_block",
    )(x1, f, g_post.reshape(1, d), gate2)


def _mm_kernel(a_ref, w_ref, o_ref, *, scale):
    acc = jnp.dot(a_ref[...], w_ref[...], preferred_element_type=F32)
    if scale != 1.0:
        acc = acc * scale
    o_ref[...] = acc.astype(o_ref.dtype)


def matmul(a, w, out_dtype, *, scale=1.0, name="matmul"):
    m, k = a.shape
    n = w.shape[1]
    tm = _row_tile(m, 1024)
    tn = _row_tile(n, 512)
    return pl.pallas_call(
        functools.partial(_mm_kernel, scale=scale),
        grid=(m // tm, n // tn),
        in_specs=[pl.BlockSpec((tm, k), lambda i, j: (i, 0)),
                  pl.BlockSpec((k, tn), lambda i, j: (0, j))],
        out_specs=pl.BlockSpec((tm, tn), lambda i, j: (i, j)),
        out_shape=jax.ShapeDtypeStruct((m, n), out_dtype),
        compiler_params=_params("parallel", "arbitrary"),
        name=name,
    )(a, w)


def _mm_acc_kernel(a_ref, w_ref, o_ref, acc_s):
    kk = pl.program_id(2)

    @pl.when(kk == 0)
    def _():
        acc_s[...] = jnp.zeros_like(acc_s)

    acc_s[...] += jnp.dot(a_ref[...], w_ref[...], preferred_element_type=F32)

    @pl.when(kk == pl.num_programs(2) - 1)
    def _():
        o_ref[...] = acc_s[...]


def matmul_ksplit(a, w, *, k_steps, name="matmul_ksplit"):
    m, k = a.shape
    n = w.shape[1]
    tm = _row_tile(m, 1024)
    tn = _row_tile(n, 512)
    tk = k // k_steps
    return pl.pallas_call(
        _mm_acc_kernel,
        grid=(m // tm, n // tn, k_steps),
        in_specs=[pl.BlockSpec((tm, tk), lambda i, j, s: (i, s)),
                  pl.BlockSpec((tk, tn), lambda i, j, s: (s, j))],
        out_specs=pl.BlockSpec((tm, tn), lambda i, j, s: (i, j)),
        out_shape=jax.ShapeDtypeStruct((m, n), F32),
        scratch_shapes=[pltpu.VMEM((tm, tn), F32)],
        compiler_params=_params("parallel", "arbitrary", "arbitrary"),
        name=name,
    )(a, w)


def _swiglu_kernel(a_ref, wg_ref, wu_ref, o_ref):
    a = a_ref[...]
    g = jnp.dot(a, wg_ref[...], preferred_element_type=F32)
    u = jnp.dot(a, wu_ref[...], preferred_element_type=F32)
    o_ref[...] = (g * jax.nn.sigmoid(g) * u).astype(o_ref.dtype)


def swiglu(a, wg, wu):
    m, k = a.shape
    n = wg.shape[1]
    tm = _row_tile(m, 1024)
    tn = _row_tile(n, 512)
    wspec = pl.BlockSpec((k, tn), lambda i, j: (0, j))
    return pl.pallas_call(
        _swiglu_kernel,
        grid=(m // tm, n // tn),
        in_specs=[pl.BlockSpec((tm, k), lambda i, j: (i, 0)), wspec, wspec],
        out_specs=pl.BlockSpec((tm, tn), lambda i, j: (i, j)),
        out_shape=jax.ShapeDtypeStruct((m, n), BF16),
        compiler_params=_params("parallel", "arbitrary"),
        name="swiglu",
    )(a, wg, wu)


def _idxkey_kernel(a_ref, w_ref, g_ref, b_ref, ki_ref, wi_ref, *, wi_scale):
    z = jnp.dot(a_ref[...], w_ref[...], preferred_element_type=F32)
    ki = z[:, :D_IDX]
    mu = jnp.mean(ki, axis=-1, keepdims=True)
    var = jnp.mean(jnp.square(ki - mu), axis=-1, keepdims=True)
    ki_ref[...] = (ki - mu) * lax.rsqrt(var + EPS) * g_ref[...] + b_ref[...]
    wi_ref[...] = z[:, D_IDX:] * wi_scale


def indexer_key_proj(a, w, ln_g, ln_b, wi_scale):
    m, k = a.shape
    n = w.shape[1]
    tm = _row_tile(m, 512)
    return pl.pallas_call(
        functools.partial(_idxkey_kernel, wi_scale=wi_scale),
        grid=(m // tm,),
        in_specs=[pl.BlockSpec((tm, k), lambda i: (i, 0)),
                  pl.BlockSpec((k, n), lambda i: (0, 0)),
                  pl.BlockSpec((1, D_IDX), lambda i: (0, 0)),
                  pl.BlockSpec((1, D_IDX), lambda i: (0, 0))],
        out_specs=[pl.BlockSpec((tm, D_IDX), lambda i: (i, 0)),
                   pl.BlockSpec((tm, n - D_IDX), lambda i: (i, 0))],
        out_shape=[jax.ShapeDtypeStruct((m, D_IDX), F32),
                   jax.ShapeDtypeStruct((m, n - D_IDX), F32)],
        compiler_params=_params("parallel"),
        name="indexer_key_proj",
    )(a, w, ln_g.reshape(1, D_IDX), ln_b.reshape(1, D_IDX))


def _rg_gates(conv, wa_ref, ba_ref, wi_ref, bi_ref, lam_ref):
    cb = conv.astype(BF16)
    r = jax.nn.sigmoid(jnp.dot(cb, wa_ref[0], preferred_element_type=F32) + ba_ref[...])
    i = jax.nn.sigmoid(jnp.dot(cb, wi_ref[0], preferred_element_type=F32) + bi_ref[...])
    nl = -lam_ref[...]
    softplus = jnp.maximum(nl, 0.0) + jnp.log1p(jnp.exp(-jnp.abs(nl)))
    log_a = -LRU_C * r * softplus
    a = jnp.exp(log_a)
    u = jnp.sqrt(-jnp.tanh(log_a) * (a * a + 1.0)) * (i * conv)
    return a, u


def _rglru_prompt_kernel(xr_ref, gt_ref, cw_ref, cb_ref, wa_ref, ba_ref, wi_ref, bi_ref, lam_ref,
                         y_ref, hlast_ref, xbuf, hc):
    tt = xr_ref.shape[0]
    hist = SUBLANES

    @pl.when(pl.program_id(1) == 0)
    def _():
        xbuf[0:hist, :] = jnp.zeros((hist, LANES), F32)
        hc[...] = jnp.zeros_like(hc)

    xbuf[hist:hist + tt, :] = xr_ref[...]
    width = cw_ref.shape[0]
    conv = cb_ref[...] + xbuf[hist - width + 1:hist - width + 1 + tt, :] * cw_ref[0:1, :]
    for j in range(1, width):
        s = hist - width + 1 + j
        conv = conv + xbuf[s:s + tt, :] * cw_ref[j:j + 1, :]
    xbuf[0:hist, :] = xbuf[tt:tt + hist, :]

    a, u = _rg_gates(conv, wa_ref, ba_ref, wi_ref, bi_ref, lam_ref)

    row = lax.broadcasted_iota(I32, (tt, LANES), 0)
    d = 1
    while d < tt:
        keep = row >= d
        a_sh = jnp.where(keep, pltpu.roll(a, d, 0), 1.0)
        u_sh = jnp.where(keep, pltpu.roll(u, d, 0), 0.0)
        u = a * u_sh + u
        a = a * a_sh
        d *= 2
    h = u + a * hc[...]
    hc[...] = h[tt - 1:tt, :]
    hlast_ref[...] = h[tt - 1:tt, :]
    y_ref[...] = h * jax.nn.gelu(gt_ref[...])


def rglru_prompt(z_rg, conv_w, conv_b, w_a, b_a, w_i, b_i, lam):
    t, c2 = z_rg.shape
    c = c2 // 2
    nb = c // LANES
    tt = _row_tile(t, 1024)
    width = conv_w.shape[0]
    vec = pl.BlockSpec((1, LANES), lambda n, i: (0, n))
    wsp = pl.BlockSpec((1, LANES, LANES), lambda n, i: (n, 0, 0))
    return pl.pallas_call(
        _rglru_prompt_kernel,
        grid=(nb, t // tt),
        in_specs=[pl.BlockSpec((tt, LANES), lambda n, i: (i, n)),
                  pl.BlockSpec((tt, LANES), lambda n, i: (i, nb + n)),
                  pl.BlockSpec((width, LANES), lambda n, i: (0, n)),
                  vec, wsp, vec, wsp, vec, vec],
        out_specs=[pl.BlockSpec((tt, LANES), lambda n, i: (i, n)), vec],
        out_shape=[jax.ShapeDtypeStruct((t, c), F32), jax.ShapeDtypeStruct((1, c), F32)],
        scratch_shapes=[pltpu.VMEM((tt + SUBLANES, LANES), F32), pltpu.VMEM((1, LANES), F32)],
        compiler_params=_params("parallel", "arbitrary"),
        name="rglru_prompt",
    )(z_rg, z_rg, conv_w, conv_b.reshape(1, c), w_a.astype(BF16), b_a.reshape(1, c),
      w_i.astype(BF16), b_i.reshape(1, c), lam.reshape(1, c))


def _rglru_sample_kernel(xr_ref, gt_ref, buf_ref, h0_ref, cw_ref, cb_ref, wa_ref, ba_ref, wi_ref,
                         bi_ref, lam_ref, y_ref, hlast_ref, hist, hc):
    width = cw_ref.shape[0]

    @pl.when(pl.program_id(1) == 0)
    def _():
        hist[...] = buf_ref[...]
        hc[...] = h0_ref[...]

    x = xr_ref[0]
    conv = cb_ref[...] + hist[0] * cw_ref[0:1, :]
    for j in range(1, width - 1):
        conv = conv + hist[j] * cw_ref[j:j + 1, :]
    conv = conv + x * cw_ref[width - 1:width, :]
    for j in range(width - 2):
        hist[j] = hist[j + 1]
    hist[width - 2] = x

    a, u = _rg_gates(conv, wa_ref, ba_ref, wi_ref, bi_ref, lam_ref)
    h = a * hc[...] + u
    hc[...] = h
    hlast_ref[...] = h
    y_ref[0] = h * jax.nn.gelu(gt_ref[0])


def rglru_sample(z_rg, buf, h0, conv_w, conv_b, w_a, b_a, w_i, b_i, lam):
    t, b, c2 = z_rg.shape
    c = c2 // 2
    nb = c // LANES
    width = conv_w.shape[0]
    vec = pl.BlockSpec((1, LANES), lambda n, i: (0, n))
    wsp = pl.BlockSpec((1, LANES, LANES), lambda n, i: (n, 0, 0))
    st = pl.BlockSpec((b, LANES), lambda n, i: (0, n))
    return pl.pallas_call(
        _rglru_sample_kernel,
        grid=(nb, t),
        in_specs=[pl.BlockSpec((1, b, LANES), lambda n, i: (i, 0, n)),
                  pl.BlockSpec((1, b, LANES), lambda n, i: (i, 0, nb + n)),
                  pl.BlockSpec((width - 1, b, LANES), lambda n, i: (0, 0, n)),
                  st,
                  pl.BlockSpec((width, LANES), lambda n, i: (0, n)),
                  vec, wsp, vec, wsp, vec, vec],
        out_specs=[pl.BlockSpec((1, b, LANES), lambda n, i: (i, 0, n)), st],
        out_shape=[jax.ShapeDtypeStruct((t, b, c), F32), jax.ShapeDtypeStruct((b, c), F32)],
        scratch_shapes=[pltpu.VMEM((width - 1, b, LANES), F32), pltpu.VMEM((b, LANES), F32)],
        compiler_params=_params("parallel", "arbitrary"),
        name="rglru_sample",
    )(z_rg, z_rg, buf, h0, conv_w, conv_b.reshape(1, c), w_a.astype(BF16), b_a.reshape(1, c),
      w_i.astype(BF16), b_i.reshape(1, c), lam.reshape(1, c))


def _t5_bucket_np(dist, n_buckets):
    max_exact = n_buckets // 2
    d = np.maximum(dist, 1).astype(np.float32)
    large = max_exact + (np.log(d / np.float32(max_exact)) / np.float32(math.log(MAX_DISTANCE / max_exact))
                         * np.float32(n_buckets - max_exact)).astype(np.int32)
    large = np.minimum(large, n_buckets - 1)
    return np.where(dist < max_exact, dist, large).astype(np.int32)


def _bias_kernel(rb_ref, bucket_ref, o_ref, *, n_buckets, far_bucket):
    h = pl.program_id(0)
    bucket = bucket_ref[...]
    shift = rb_ref[far_bucket, h] if far_bucket is not None else 0.0
    acc = jnp.zeros(bucket.shape, F32)
    for b in range(n_buckets):
        acc = jnp.where(bucket == b, rb_ref[b, h] - shift, acc)
    o_ref[0] = acc


def bias_tiles(rel_bias, bucket, far_bucket):
    n_buckets, n_heads = rel_bias.shape
    r, c = bucket.shape
    return pl.pallas_call(
        functools.partial(_bias_kernel, n_buckets=n_buckets, far_bucket=far_bucket),
        grid=(n_heads,),
        in_specs=[pl.BlockSpec(memory_space=pltpu.SMEM),
                  pl.BlockSpec((r, c), lambda h: (0, 0))],
        out_specs=pl.BlockSpec((1, r, c), lambda h: (h, 0, 0)),
        out_shape=jax.ShapeDtypeStruct((n_heads, r, c), F32),
        compiler_params=_params("parallel"),
        name="bias_tiles",
    )(rel_bias, jnp.asarray(bucket))


TQ = 128
TK_NEAR = 128
TK_FAR = 512
FAR_PER_NEAR = TK_FAR // TK_NEAR


def _pattn_kernel(qi_ref, wit_ref, q_ref, kidx_ref, k_ref, vt_ref, bias_ref, o_ref,
                  keys_s, qis_s, qs_s, acc_s, m_s, l_s, *, seq_len, n_sel, gqa):
    i = pl.program_id(0)
    n_idx = qis_s.shape[0] // TQ
    gq = gqa * TQ

    for h in range(n_idx):
        qis_s[h * TQ:(h + 1) * TQ, :] = qi_ref[:, h * D_IDX:(h + 1) * D_IDX]
    for kv in range(N_KV):
        for g in range(gqa):
            hh = kv * gqa + g
            qs_s[kv, g * TQ:(g + 1) * TQ, :] = q_ref[:, hh * HEAD_DIM:(hh + 1) * HEAD_DIM]

    qpos = i * TQ + lax.broadcasted_iota(I32, (1, TQ), 1)
    n_far = jnp.maximum(i - 1, 0) // FAR_PER_NEAR

    def score_tile(k0, tk, near):
        kt = kidx_ref[pl.ds(k0, tk), :]
        acc = jnp.zeros((tk, TQ), F32)
        for hp in range(n_idx // 2):
            s = lax.dot_general(kt, qis_s[hp * 2 * TQ:(hp + 1) * 2 * TQ, :], NT_DIMS,
                                preferred_element_type=F32)
            for e in range(2):
                h = 2 * hp + e
                acc = acc + jnp.maximum(s[:, e * TQ:(e + 1) * TQ], 0.0) * wit_ref[h:h + 1, :]
        if near:
            kpos = k0 + lax.broadcasted_iota(I32, (tk, 1), 0)
            acc = jnp.where(kpos <= qpos, acc, NEG)
        keys_s[pl.ds(k0, tk), :] = _sortable_keys(acc)

    def far_scores(j, c):
        score_tile(pl.multiple_of(j * TK_FAR, TK_FAR), TK_FAR, False)
        return c

    def near_scores(j, c):
        score_tile(pl.multiple_of(j * TK_NEAR, TK_NEAR), TK_NEAR, True)
        return c

    lax.fori_loop(0, n_far, far_scores, 0)
    lax.fori_loop(n_far * FAR_PER_NEAR, i + 1, near_scores, 0)

    n_after = seq_len - (i + 1) * TQ

    def bit_step(b, thr):
        cand = thr ^ lax.shift_left(jnp.int32(1), 31 - b)

        def chunk(c, cnt):
            u = keys_s[pl.ds(pl.multiple_of(c * TK_NEAR, TK_NEAR), TK_NEAR), :]
            ge = u.reshape(TK_NEAR // SUBLANES, SUBLANES, TQ) >= cand[None]
            return cnt + jnp.sum(jnp.where(ge, 1, 0), axis=0)

        cnt = lax.fori_loop(0, i + 1, chunk, jnp.zeros((SUBLANES, TQ), I32))
        tot = jnp.sum(cnt, axis=0, keepdims=True)
        tot = tot + jnp.where(cand[0:1] <= NEG_KEY, n_after, 0)
        return jnp.where(tot >= n_sel, cand, thr)

    thr = lax.fori_loop(0, 32, bit_step, jnp.full((SUBLANES, TQ), INT_MIN, I32))
    thr = jnp.maximum(thr[0:1], INT_MIN + 1)
    thr_g = jnp.concatenate([thr] * gqa, axis=1)

    m_s[...] = jnp.full(m_s.shape, NEG, F32)
    l_s[...] = jnp.zeros(l_s.shape, F32)
    acc_s[...] = jnp.zeros(acc_s.shape, F32)

    def attn_tile(k0, tk, near, delta):
        u = keys_s[pl.ds(k0, tk), :]
        if near:
            kpos = k0 + lax.broadcasted_iota(I32, (tk, 1), 0)
            u = jnp.where(kpos <= qpos, u, INT_MIN)
        sel = jnp.concatenate([u] * gqa, axis=1) >= thr_g
        for kv in range(N_KV):
            kt = k_ref[pl.ds(k0, tk), kv * HEAD_DIM:(kv + 1) * HEAD_DIM]
            s = lax.dot_general(kt, qs_s[kv], NT_DIMS, preferred_element_type=F32)
            if near:
                s = s + bias_ref[delta, kv]
            s = jnp.where(sel, s, -jnp.inf)
            m_old = m_s[kv]
            m_new = jnp.maximum(m_old, jnp.max(s, axis=0, keepdims=True))
            alpha = jnp.exp(m_old - m_new)
            p = jnp.exp(s - m_new)
            l_s[kv] = alpha * l_s[kv] + jnp.sum(p, axis=0, keepdims=True)
            vt = vt_ref[kv * HEAD_DIM:(kv + 1) * HEAD_DIM, pl.ds(k0, tk)]
            pv = jnp.dot(vt, p.astype(BF16), preferred_element_type=F32)
            acc_s[kv] = acc_s[kv] * alpha + pv
            m_s[kv] = m_new

    def far_attn(j, c):
        attn_tile(pl.multiple_of(j * TK_FAR, TK_FAR), TK_FAR, False, None)
        return c

    def near_attn(j, c):
        attn_tile(pl.multiple_of(j * TK_NEAR, TK_NEAR), TK_NEAR, True, jnp.minimum(i - j, 2))
        return c

    lax.fori_loop(0, n_far, far_attn, 0)
    lax.fori_loop(n_far * FAR_PER_NEAR, i + 1, near_attn, 0)

    for kv in range(N_KV):
        ot = acc_s[kv] * (1.0 / l_s[kv])
        for g in range(gqa):
            hh = kv * gqa + g
            o_ref[:, hh * HEAD_DIM:(hh + 1) * HEAD_DIM] = ot[:, g * TQ:(g + 1) * TQ].T


def prompt_attention(q, qi, wi_t, kidx, k, v_t, bias, n_sel):
    t, dq = q.shape
    n_heads = dq // HEAD_DIM
    gqa = n_heads // N_KV
    n_idx = qi.shape[1] // D_IDX
    dkv = N_KV * HEAD_DIM
    whole = lambda shape: pl.BlockSpec(shape, lambda i: (0,) * len(shape))
    return pl.pallas_call(
        functools.partial(_pattn_kernel, seq_len=t, n_sel=n_sel, gqa=gqa),
        grid=(t // TQ,),
        in_specs=[pl.BlockSpec((TQ, n_idx * D_IDX), lambda i: (i, 0)),
                  pl.BlockSpec((n_idx, TQ), lambda i: (0, i)),
                  pl.BlockSpec((TQ, dq), lambda i: (i, 0)),
                  whole((t, D_IDX)), whole((t, dkv)), whole((dkv, t)),
                  whole(bias.shape)],
        out_specs=pl.BlockSpec((TQ, dq), lambda i: (i, 0)),
        out_shape=jax.ShapeDtypeStruct((t, dq), F32),
        scratch_shapes=[pltpu.VMEM((t, TQ), I32),
                        pltpu.VMEM((n_idx * TQ, D_IDX), BF16),
                        pltpu.VMEM((N_KV, gqa * TQ, HEAD_DIM), BF16),
                        pltpu.VMEM((N_KV, HEAD_DIM, gqa * TQ), F32),
                        pltpu.VMEM((N_KV, 1, gqa * TQ), F32),
                        pltpu.VMEM((N_KV, 1, gqa * TQ), F32)],
        compiler_params=_params("parallel"),
        name="prompt_attention",
    )(qi, wi_t, q, kidx, k, v_t, bias)


def _sattn_kernel(pt_ref, qi_ref, wi_ref, q_ref, kin_ref, kn_ref, vn_ref, bias_ref, *rest,
                  n_pages, page, n_sel, gqa):
    ki_pages = rest[:n_pages]
    k_pages = rest[n_pages:2 * n_pages]
    v_pages = rest[2 * n_pages:3 * n_pages]
    o_ref, kibuf, kbuf, vbuf = rest[3 * n_pages:]
    del pt_ref
    past = n_pages * page
    lp = kibuf.shape[0]
    t_new = o_ref.shape[1]
    n_new = kin_ref.shape[1]
    rows = q_ref.shape[1]
    dkv = N_KV * HEAD_DIM

    for p in range(n_pages):
        kibuf[p * page:(p + 1) * page, :] = ki_pages[p][0].astype(BF16)
        kbuf[p * page:(p + 1) * page, :] = k_pages[p][0].astype(BF16)
        vbuf[p * page:(p + 1) * page, :] = v_pages[p][0].astype(BF16)
    kibuf[past:, :] = jnp.zeros((lp - past, D_IDX), BF16)
    kbuf[past:, :] = jnp.zeros((lp - past, dkv), BF16)
    vbuf[past:, :] = jnp.zeros((lp - past, dkv), BF16)
    kibuf[past:past + n_new, :] = kin_ref[0]
    kbuf[past:past + n_new, :] = kn_ref[0]
    vbuf[past:past + n_new, :] = vn_ref[0]

    kpos = lax.broadcasted_iota(I32, (1, lp), 1)
    qpos = past + lax.broadcasted_iota(I32, (t_new, 1), 0)
    visible = kpos <= qpos

    s = lax.dot_general(qi_ref[0], kibuf[...], NT_DIMS, preferred_element_type=F32)
    s = jnp.maximum(s, 0.0) * wi_ref[0]
    n_idx = s.shape[0] // t_new
    score = jnp.sum(s.reshape(n_idx, t_new, lp), axis=0)
    score = jnp.where(visible, score, NEG)
    keys = jnp.where(kpos < past + t_new, _sortable_keys(score), INT_MIN)

    def bit_step(b, thr):
        cand = thr ^ lax.shift_left(jnp.int32(1), 31 - b)
        tot = jnp.sum(jnp.where(keys >= cand, 1, 0), axis=1, keepdims=True)
        return jnp.where(tot >= n_sel, cand, thr)

    thr = lax.fori_loop(0, 32, bit_step, jnp.full((t_new, 1), INT_MIN, I32))
    thr = jnp.maximum(thr, INT_MIN + 1)
    sel = jnp.where(visible, keys, INT_MIN) >= thr

    qrep = jnp.concatenate([q_ref[0]] * N_KV, axis=1)
    rkv = lax.broadcasted_iota(I32, (rows, dkv), 0) // (gqa * t_new)
    ckv = lax.broadcasted_iota(I32, (rows, dkv), 1) // HEAD_DIM
    qblk = jnp.where(rkv == ckv, qrep, jnp.zeros_like(qrep))
    logits = lax.dot_general(qblk, kbuf[...], NT_DIMS, preferred_element_type=F32) + bias_ref[...]
    n_heads = rows // t_new
    logits = jnp.where(sel[None], logits.reshape(n_heads, t_new, lp), -jnp.inf)
    m = jnp.maximum(jnp.max(logits, axis=-1, keepdims=True), NEG)
    p = jnp.exp(logits - m)
    l = jnp.sum(p, axis=-1, keepdims=True)
    o = jnp.dot(p.reshape(rows, lp).astype(BF16), vbuf[...], preferred_element_type=F32)
    o = o.reshape(n_heads, t_new, dkv) * (1.0 / l)
    for kv in range(N_KV):
        for g in range(gqa):
            hh = kv * gqa + g
            o_ref[0, :, hh * HEAD_DIM:(hh + 1) * HEAD_DIM] = o[hh, :, kv * HEAD_DIM:(kv + 1) * HEAD_DIM]


def sample_attention(page_table, qi, wi, q, ki_new, k_new, v_new, bias, cache_ki, cache_k, cache_v,
                     t_new, n_sel):
    b, rows, _ = q.shape
    n_heads = rows // t_new
    gqa = n_heads // N_KV
    n_pages = page_table.shape[1]
    page = cache_k.shape[1]
    dkv = N_KV * HEAD_DIM
    lp = bias.shape[1]
    n_new = ki_new.shape[1]
    per_seq = lambda shape: pl.BlockSpec((1,) + shape, lambda s, pt: (s, 0, 0))

    def page_spec(width, p):
        return pl.BlockSpec((1, page, width), lambda s, pt: (pt[s, p], 0, 0))

    in_specs = [per_seq((qi.shape[1], D_IDX)), per_seq((wi.shape[1], 1)), per_seq((rows, HEAD_DIM)),
                per_seq((n_new, D_IDX)), per_seq((n_new, dkv)), per_seq((n_new, dkv)),
                pl.BlockSpec((rows, lp), lambda s, pt: (0, 0))]
    in_specs += [page_spec(D_IDX, p) for p in range(n_pages)]
    in_specs += [page_spec(dkv, p) for p in range(n_pages)]
    in_specs += [page_spec(dkv, p) for p in range(n_pages)]
    grid_spec = pltpu.PrefetchScalarGridSpec(
        num_scalar_prefetch=1,
        grid=(b,),
        in_specs=in_specs,
        out_specs=pl.BlockSpec((1, t_new, n_heads * HEAD_DIM), lambda s, pt: (s, 0, 0)),
        scratch_shapes=[pltpu.VMEM((lp, D_IDX), BF16), pltpu.VMEM((lp, dkv), BF16),
                        pltpu.VMEM((lp, dkv), BF16)],
    )
    return pl.pallas_call(
        functools.partial(_sattn_kernel, n_pages=n_pages, page=page, n_sel=n_sel, gqa=gqa),
        grid_spec=grid_spec,
        out_shape=jax.ShapeDtypeStruct((b, t_new, n_heads * HEAD_DIM), F32),
        compiler_params=_params("parallel"),
        name="sample_attention",
    )(page_table, qi, wi, q, ki_new, k_new, v_new, bias,
      *([cache_ki] * n_pages), *([cache_k] * n_pages), *([cache_v] * n_pages))


def _pad_cols(w, n):
    return jnp.pad(w, ((0, 0), (0, n - w.shape[1])))


def _prompt_bias(rel_bias, gqa):
    n_buckets = rel_bias.shape[0]
    kl = np.arange(TK_NEAR)[:, None]
    ql = np.arange(TQ)[None, :]
    dist = np.stack([np.maximum(d * TK_NEAR + ql - kl, 0) for d in range(2)])
    bucket = _t5_bucket_np(dist, n_buckets).reshape(2 * TK_NEAR, TQ)
    far = int(_t5_bucket_np(np.array([MAX_DISTANCE + 1]), n_buckets)[0])
    tiles = bias_tiles(rel_bias, bucket, far)
    tiles = tiles.reshape(N_KV, gqa, 2, TK_NEAR, TQ).transpose(2, 0, 3, 1, 4)
    tiles = tiles.reshape(2, N_KV, TK_NEAR, gqa * TQ)
    return jnp.concatenate([tiles, jnp.zeros_like(tiles[:1])], axis=0)


def _sample_bias(rel_bias, past, t_new, lp):
    n_buckets, n_heads = rel_bias.shape
    dist = np.maximum(past + np.arange(t_new)[:, None] - np.arange(lp)[None, :], 0)
    tiles = bias_tiles(rel_bias, _t5_bucket_np(dist, n_buckets), None)
    return tiles.reshape(n_heads * t_new, lp)


def _mixer_in(x, mods, g_pre_mix, w, ln_g, ln_b, sizes):
    d_rnn, d_q, d_kv, d_qi = sizes
    h1 = prenorm(x, g_pre_mix, mods[1], mods[0])
    z_rg = matmul(h1, w["rg"], F32, name="in_proj_rg")
    q = matmul(h1, w["q"], BF16, scale=HEAD_DIM ** -0.5, name="in_proj_q")
    kv = matmul(h1, w["kv"], F32, name="in_proj_kv")
    qi = matmul(h1, w["qi"], BF16, name="in_proj_qi")
    n_idx = d_qi // D_IDX
    ki, wi = indexer_key_proj(h1, w["kiwi"], ln_g, ln_b, float(d_qi) ** -0.5)
    return z_rg, q, kv[:, :d_kv], kv[:, d_kv:], qi, ki, wi[:, :n_idx]


def _block_out(x, y_rnn, y_attn, mods, w, g_out_rnn, g_out_attn, g_post_mix, g_pre_ffn, g_post_ffn):
    ycat = group_norm_concat(y_rnn, y_attn, g_out_rnn, g_out_attn)
    mixed = matmul(ycat, w["out"], F32, name="out_proj")
    x1, h2 = mid_block(x, mixed, g_post_mix, mods[2], g_pre_ffn, mods[4], mods[3])
    hf = swiglu(h2, w["ffn_gate"], w["ffn_up"])
    f = matmul_ksplit(hf, w["ffn_down"], k_steps=2, name="ffn_down")
    return final_block(x1, f, g_post_ffn, mods[5])


def kernel(x_prompt, x_sample, cache_k, cache_v, cache_idx_k, state_conv, state_h, page_table,
           c_prompt, c_sample, rel_bias, w_mod, b_mod, g_pre_mix, g_post_mix, g_pre_ffn, g_post_ffn,
           w_in, conv_w, conv_b, w_rg_a, b_rg_a, w_rg_i, b_rg_i, lru_lambda, idx_k_ln_g, idx_k_ln_b,
           g_out_rnn, g_out_attn, w_out, w_ffn_gate, w_ffn_up, w_ffn_down):
    depth = w_in.shape[0]
    assert depth == 1, "one layer: prompt/sample caches are not threaded through deeper stacks"
    bp, t_p, d = x_prompt.shape
    assert bp == 1, "the prompt group is one sequence"
    bs, t_s, _ = x_sample.shape
    d_rnn = w_rg_a.shape[1] * w_rg_a.shape[2]
    d_attn = d - d_rnn
    n_heads = d_attn // HEAD_DIM
    gqa = n_heads // N_KV
    d_kv = N_KV * HEAD_DIM
    d_qi = N_IDX_HEADS * D_IDX
    n_pages = page_table.shape[1]
    page = cache_k.shape[2]
    past = n_pages * page
    width = conv_w.shape[1]
    d_ff = w_ffn_gate.shape[2]
    d_ff_pad = _round_up(d_ff, 1024)
    sizes = (d_rnn, d_attn, d_kv, d_qi)

    w_in0 = w_in[0]
    c0 = 2 * d_rnn
    c1 = c0 + d_attn
    c2 = c1 + 2 * d_kv
    c3 = c2 + d_qi
    w = {
        "rg": w_in0[:, :c0].astype(BF16),
        "q": w_in0[:, c0:c1].astype(BF16),
        "kv": w_in0[:, c1:c2].astype(BF16),
        "qi": w_in0[:, c2:c3].astype(BF16),
        "kiwi": _pad_cols(w_in0[:, c3:], D_IDX + LANES).astype(BF16),
        "out": w_out[0].astype(BF16),
        "ffn_gate": _pad_cols(w_ffn_gate[0], d_ff_pad).astype(BF16),
        "ffn_up": _pad_cols(w_ffn_up[0], d_ff_pad).astype(BF16),
        "ffn_down": jnp.pad(w_ffn_down[0], ((0, d_ff_pad - d_ff), (0, 0))).astype(BF16),
    }

    n_c = _round_up(bs + 1, BF16_ROWS)
    c_all = jnp.concatenate([c_sample, c_prompt, jnp.zeros((n_c - bs - 1, d), F32)], axis=0)
    mod_all = modulation(c_all, w_mod[0], b_mod[0])
    mods_p = [mod_all[bs:bs + 1, j * d:(j + 1) * d] for j in range(6)]
    mods_s = [jnp.repeat(mod_all[:bs, j * d:(j + 1) * d], t_s, axis=0) for j in range(6)]

    rnn_w = (conv_w[0], conv_b[0], w_rg_a[0], b_rg_a[0], w_rg_i[0], b_rg_i[0], lru_lambda[0])
    out_w = (g_out_rnn[0], g_out_attn[0], g_post_mix[0], g_pre_ffn[0], g_post_ffn[0])

    xp = x_prompt.reshape(t_p, d)
    z_rg, q, k, v, qi, ki, wi = _mixer_in(xp, mods_p, g_pre_mix[0], w, idx_k_ln_g[0], idx_k_ln_b[0], sizes)
    y_rnn, h_p = rglru_prompt(z_rg, *rnn_w)
    n_sel_p = min(TOPK_MAX, t_p // 4)
    y_attn = prompt_attention(q, qi, wi.T, ki.astype(BF16), k.astype(BF16), v.astype(BF16).T,
                              _prompt_bias(rel_bias, gqa), n_sel_p)
    y_p = _block_out(xp, y_rnn, y_attn, mods_p, w, *out_w)
    k_prompt = k.reshape(1, 1, t_p, N_KV, HEAD_DIM)
    v_prompt = v.reshape(1, 1, t_p, N_KV, HEAD_DIM)
    kidx_prompt = ki.reshape(1, 1, t_p, D_IDX)
    conv_prompt = z_rg[t_p - (width - 1):, :d_rnn].reshape(1, 1, width - 1, d_rnn)
    h_prompt = h_p.reshape(1, 1, d_rnn)

    m_s = bs * t_s
    xs = x_sample.reshape(m_s, d)
    z_rg, q, k, v, qi, ki, wi = _mixer_in(xs, mods_s, g_pre_mix[0], w, idx_k_ln_g[0], idx_k_ln_b[0], sizes)
    z_t = z_rg.reshape(bs, t_s, 2 * d_rnn).transpose(1, 0, 2)
    y_t, h_s = rglru_sample(z_t, state_conv[0].transpose(1, 0, 2), state_h[0], *rnn_w)
    y_rnn = y_t.transpose(1, 0, 2).reshape(m_s, d_rnn)

    def head_major(a, n):
        return a.reshape(bs, t_s, n, a.shape[1] // n).transpose(0, 2, 1, 3).reshape(bs, n * t_s, -1)

    n_new = _round_up(t_s, BF16_ROWS)

    def new_rows(a):
        a = a.reshape(bs, t_s, -1).astype(BF16)
        return jnp.pad(a, ((0, 0), (0, n_new - t_s), (0, 0)))

    lp = past + _round_up(t_s, LANES)
    n_sel_s = min(TOPK_MAX, (past + t_s) // 4)
    wi_rows = wi.reshape(bs, t_s, N_IDX_HEADS).transpose(0, 2, 1).reshape(bs, N_IDX_HEADS * t_s, 1)
    y_attn = sample_attention(
        page_table, head_major(qi, N_IDX_HEADS), wi_rows, head_major(q, n_heads),
        new_rows(ki), new_rows(k), new_rows(v), _sample_bias(rel_bias, past, t_s, lp),
        cache_idx_k[0], cache_k[0].reshape(-1, page, d_kv), cache_v[0].reshape(-1, page, d_kv),
        t_s, n_sel_s).reshape(m_s, d_attn)
    y_s = _block_out(xs, y_rnn, y_attn, mods_s, w, *out_w)
    k_sample = k.reshape(1, bs, t_s, N_KV, HEAD_DIM)
    v_sample = v.reshape(1, bs, t_s, N_KV, HEAD_DIM)
    kidx_sample = ki.reshape(1, bs, t_s, D_IDX)
    xr_s = z_rg[:, :d_rnn].reshape(bs, t_s, d_rnn)
    hist = jnp.concatenate([state_conv[0], xr_s], axis=1)[:, t_s:]
    conv_sample = hist.reshape(1, bs, width - 1, d_rnn)
    h_sample = h_s.reshape(1, bs, d_rnn)

    return (y_p.reshape(1, t_p, d), y_s.reshape(bs, t_s, d), k_prompt, v_prompt, kidx_prompt,
            conv_prompt, h_prompt, k_sample, v_sample, kidx_sample, conv_sample, h_sample)
```

```python
import functools
import math

import jax
import jax.numpy as jnp
import numpy as np
from jax import lax
from jax.experimental import pallas as pl
from jax.experimental.pallas import tpu as pltpu

HEAD_DIM = 128
N_KV = 4
N_IDX_HEADS = 16
D_IDX = 128
TOPK_MAX = 256
MAX_DISTANCE = 128
LRU_C = 8.0
EPS = 1e-6
NEG = -1e30

LANES = 128
SUBLANES = 8
BF16_ROWS = 16
VMEM_LIMIT_BYTES = 56 * 1024 * 1024
ROWWISE_TILE = 128

F32 = jnp.float32
BF16 = jnp.bfloat16
I32 = jnp.int32
INT_MIN = -(2 ** 31)
NT_DIMS = (((1,), (1,)), ((), ()))


def _sortable_key_of(x):
    b = int(np.array(x, np.float32).view(np.int32))
    return b ^ ((b >> 31) & 0x7FFFFFFF)


NEG_KEY = _sortable_key_of(NEG)


def _params(*semantics):
    return pltpu.CompilerParams(dimension_semantics=semantics, vmem_limit_bytes=VMEM_LIMIT_BYTES)


def _round_up(x, m):
    return -(-x // m) * m


def _row_tile(m, cap):
    t = min(m, cap)
    while m % t:
        t //= 2
    return t


def _sortable_keys(x):
    bits = pltpu.bitcast(x, I32)
    return bits ^ ((bits >> 31) & 0x7FFFFFFF)


def _rms(x, g):
    return x * lax.rsqrt(jnp.mean(x * x, axis=-1, keepdims=True) + EPS) * g


def _mod_kernel(c_ref, w_ref, b_ref, o_ref, a_s):
    @pl.when(pl.program_id(0) == 0)
    def _():
        c = c_ref[...]
        a_s[...] = (c * jax.nn.sigmoid(c)).astype(BF16)

    acc = jnp.dot(a_s[...], w_ref[...].astype(BF16), preferred_element_type=F32)
    o_ref[...] = acc + b_ref[...]


def modulation(c, w_mod, b_mod):
    b, d = c.shape
    n = w_mod.shape[1]
    tn = 512
    return pl.pallas_call(
        _mod_kernel,
        grid=(n // tn,),
        in_specs=[pl.BlockSpec((b, d), lambda j: (0, 0)),
                  pl.BlockSpec((d, tn), lambda j: (0, j)),
                  pl.BlockSpec((1, tn), lambda j: (0, j))],
        out_specs=pl.BlockSpec((b, tn), lambda j: (0, j)),
        out_shape=jax.ShapeDtypeStruct((b, n), F32),
        scratch_shapes=[pltpu.VMEM((b, d), BF16)],
        compiler_params=_params("arbitrary"),
        name="modulation",
    )(c, w_mod, b_mod.reshape(1, n))


def _mod_spec(mod, tm, d):
    if mod.shape[0] == 1:
        return pl.BlockSpec((1, d), lambda i: (0, 0))
    return pl.BlockSpec((tm, d), lambda i: (i, 0))


def _prenorm_kernel(x_ref, g_ref, sc_ref, sh_ref, o_ref):
    y = _rms(x_ref[...], g_ref[...])
    o_ref[...] = (y * (1.0 + sc_ref[...]) + sh_ref[...]).astype(o_ref.dtype)


def prenorm(x, g, scale, shift):
    m, d = x.shape
    tm = _row_tile(m, ROWWISE_TILE)
    return pl.pallas_call(
        _prenorm_kernel,
        grid=(m // tm,),
        in_specs=[pl.BlockSpec((tm, d), lambda i: (i, 0)),
                  pl.BlockSpec((1, d), lambda i: (0, 0)),
                  _mod_spec(scale, tm, d), _mod_spec(shift, tm, d)],
        out_specs=pl.BlockSpec((tm, d), lambda i: (i, 0)),
        out_shape=jax.ShapeDtypeStruct((m, d), BF16),
        compiler_params=_params("parallel"),
        name="prenorm",
    )(x, g.reshape(1, d), scale, shift)


def _group_norm_kernel(a_ref, b_ref, ga_ref, gb_ref, o_ref):
    da = a_ref.shape[1]
    o_ref[:, :da] = _rms(a_ref[...], ga_ref[...]).astype(o_ref.dtype)
    o_ref[:, da:] = _rms(b_ref[...], gb_ref[...]).astype(o_ref.dtype)


def group_norm_concat(a, b, ga, gb):
    m, da = a.shape
    db = b.shape[1]
    tm = _row_tile(m, ROWWISE_TILE)
    return pl.pallas_call(
        _group_norm_kernel,
        grid=(m // tm,),
        in_specs=[pl.BlockSpec((tm, da), lambda i: (i, 0)),
                  pl.BlockSpec((tm, db), lambda i: (i, 0)),
                  pl.BlockSpec((1, da), lambda i: (0, 0)),
                  pl.BlockSpec((1, db), lambda i: (0, 0))],
        out_specs=pl.BlockSpec((tm, da + db), lambda i: (i, 0)),
        out_shape=jax.ShapeDtypeStruct((m, da + db), BF16),
        compiler_params=_params("parallel"),
        name="group_norm_concat",
    )(a, b, ga.reshape(1, da), gb.reshape(1, db))


def _mid_kernel(x_ref, mix_ref, gpost_ref, gate_ref, gpre_ref, sc_ref, sh_ref, x1_ref, h_ref):
    x1 = x_ref[...] + gate_ref[...] * _rms(mix_ref[...], gpost_ref[...])
    x1_ref[...] = x1
    h_ref[...] = (_rms(x1, gpre_ref[...]) * (1.0 + sc_ref[...]) + sh_ref[...]).astype(h_ref.dtype)


def mid_block(x, mixed, g_post, gate1, g_pre, scale2, shift2):
    m, d = x.shape
    tm = _row_tile(m, ROWWISE_TILE)
    row = pl.BlockSpec((tm, d), lambda i: (i, 0))
    vec = pl.BlockSpec((1, d), lambda i: (0, 0))
    return pl.pallas_call(
        _mid_kernel,
        grid=(m // tm,),
        in_specs=[row, row, vec, _mod_spec(gate1, tm, d), vec,
                  _mod_spec(scale2, tm, d), _mod_spec(shift2, tm, d)],
        out_specs=[row, row],
        out_shape=[jax.ShapeDtypeStruct((m, d), F32), jax.ShapeDtypeStruct((m, d), BF16)],
        compiler_params=_params("parallel"),
        name="mid_block",
    )(x, mixed, g_post.reshape(1, d), gate1, g_pre.reshape(1, d), scale2, shift2)


def _final_kernel(x_ref, f_ref, g_ref, gate_ref, o_ref):
    o_ref[...] = x_ref[...] + gate_ref[...] * _rms(f_ref[...], g_ref[...])


def final_block(x1, f, g_post, gate2):
    m, d = x1.shape
    tm = _row_tile(m, ROWWISE_TILE)
    row = pl.BlockSpec((tm, d), lambda i: (i, 0))
    return pl.pallas_call(
        _final_kernel,
        grid=(m // tm,),
        in_specs=[row, row, pl.BlockSpec((1, d), lambda i: (0, 0)), _mod_spec(gate2, tm, d)],
        out_specs=row,
        out_shape=jax.ShapeDtypeStruct((m, d), F32),
        compiler_params=_params("parallel"),
        name="final_block",
    )(x1, f, g_post.reshape(1, d), gate2)


def _mm_kernel(a_ref, w_ref, o_ref, *, scale):
    acc = jnp.dot(a_ref[...], w_ref[...], preferred_element_type=F32)
    if scale != 1.0:
        acc = acc * scale
    o_ref[...] = acc.astype(o_ref.dtype)


def matmul(a, w, out_dtype, *, scale=1.0, name="matmul"):
    m, k = a.shape
    n = w.shape[1]
    tm = _row_tile(m, 1024)
    tn = _row_tile(n, 512)
    return pl.pallas_call(
        functools.partial(_mm_kernel, scale=scale),
        grid=(m // tm, n // tn),
        in_specs=[pl.BlockSpec((tm, k), lambda i, j: (i, 0)),
                  pl.BlockSpec((k, tn), lambda i, j: (0, j))],
        out_specs=pl.BlockSpec((tm, tn), lambda i, j: (i, j)),
        out_shape=jax.ShapeDtypeStruct((m, n), out_dtype),
        compiler_params=_params("parallel", "arbitrary"),
        name=name,
    )(a, w)


def _mm_acc_kernel(a_ref, w_ref, o_ref, acc_s):
    kk = pl.program_id(2)

    @pl.when(kk == 0)
    def _():
        acc_s[...] = jnp.zeros_like(acc_s)

    acc_s[...] += jnp.dot(a_ref[...], w_ref[...], preferred_element_type=F32)

    @pl.when(kk == pl.num_programs(2) - 1)
    def _():
        o_ref[...] = acc_s[...]


def matmul_ksplit(a, w, *, k_steps, name="matmul_ksplit"):
    m, k = a.shape
    n = w.shape[1]
    tm = _row_tile(m, 1024)
    tn = _row_tile(n, 512)
    tk = k // k_steps
    return pl.pallas_call(
        _mm_acc_kernel,
        grid=(m // tm, n // tn, k_steps),
        in_specs=[pl.BlockSpec((tm, tk), lambda i, j, s: (i, s)),
                  pl.BlockSpec((tk, tn), lambda i, j, s: (s, j))],
        out_specs=pl.BlockSpec((tm, tn), lambda i, j, s: (i, j)),
        out_shape=jax.ShapeDtypeStruct((m, n), F32),
        scratch_shapes=[pltpu.VMEM((tm, tn), F32)],
        compiler_params=_params("parallel", "arbitrary", "arbitrary"),
        name=name,
    )(a, w)


def _swiglu_kernel(a_ref, wg_ref, wu_ref, o_ref):
    a = a_ref[...]
    g = jnp.dot(a, wg_ref[...], preferred_element_type=F32)
    u = jnp.dot(a, wu_ref[...], preferred_element_type=F32)
    o_ref[...] = (g * jax.nn.sigmoid(g) * u).astype(o_ref.dtype)


def swiglu(a, wg, wu):
    m, k = a.shape
    n = wg.shape[1]
    tm = _row_tile(m, 1024)
    tn = _row_tile(n, 512)
    wspec = pl.BlockSpec((k, tn), lambda i, j: (0, j))
    return pl.pallas_call(
        _swiglu_kernel,
        grid=(m // tm, n // tn),
        in_specs=[pl.BlockSpec((tm, k), lambda i, j: (i, 0)), wspec, wspec],
        out_specs=pl.BlockSpec((tm, tn), lambda i, j: (i, j)),
        out_shape=jax.ShapeDtypeStruct((m, n), BF16),
        compiler_params=_params("parallel", "arbitrary"),
        name="swiglu",
    )(a, wg, wu)


def _idxkey_kernel(a_ref, w_ref, g_ref, b_ref, ki_ref, wi_ref, *, wi_scale):
    z = jnp.dot(a_ref[...], w_ref[...], preferred_element_type=F32)
    ki = z[:, :D_IDX]
    mu = jnp.mean(ki, axis=-1, keepdims=True)
    var = jnp.mean(jnp.square(ki - mu), axis=-1, keepdims=True)
    ki_ref[...] = (ki - mu) * lax.rsqrt(var + EPS) * g_ref[...] + b_ref[...]
    wi_ref[...] = z[:, D_IDX:] * wi_scale


def indexer_key_proj(a, w, ln_g, ln_b, wi_scale):
    m, k = a.shape
    n = w.shape[1]
    tm = _row_tile(m, 512)
    return pl.pallas_call(
        functools.partial(_idxkey_kernel, wi_scale=wi_scale),
        grid=(m // tm,),
        in_specs=[pl.BlockSpec((tm, k), lambda i: (i, 0)),
                  pl.BlockSpec((k, n), lambda i: (0, 0)),
                  pl.BlockSpec((1, D_IDX), lambda i: (0, 0)),
                  pl.BlockSpec((1, D_IDX), lambda i: (0, 0))],
        out_specs=[pl.BlockSpec((tm, D_IDX), lambda i: (i, 0)),
                   pl.BlockSpec((tm, n - D_IDX), lambda i: (i, 0))],
        out_shape=[jax.ShapeDtypeStruct((m, D_IDX), F32),
                   jax.ShapeDtypeStruct((m, n - D_IDX), F32)],
        compiler_params=_params("parallel"),
        name="indexer_key_proj",
    )(a, w, ln_g.reshape(1, D_IDX), ln_b.reshape(1, D_IDX))


def _rg_gates(conv, wa_ref, ba_ref, wi_ref, bi_ref, lam_ref):
    cb = conv.astype(BF16)
    r = jax.nn.sigmoid(jnp.dot(cb, wa_ref[0], preferred_element_type=F32) + ba_ref[...])
    i = jax.nn.sigmoid(jnp.dot(cb, wi_ref[0], preferred_element_type=F32) + bi_ref[...])
    nl = -lam_ref[...]
    softplus = jnp.maximum(nl, 0.0) + jnp.log1p(jnp.exp(-jnp.abs(nl)))
    log_a = -LRU_C * r * softplus
    a = jnp.exp(log_a)
    u = jnp.sqrt(-jnp.tanh(log_a) * (a * a + 1.0)) * (i * conv)
    return a, u


def _rglru_prompt_kernel(xr_ref, gt_ref, cw_ref, cb_ref, wa_ref, ba_ref, wi_ref, bi_ref, lam_ref,
                         y_ref, hlast_ref, xbuf, hc):
    tt = xr_ref.shape[0]
    hist = SUBLANES

    @pl.when(pl.program_id(1) == 0)
    def _():
        xbuf[0:hist, :] = jnp.zeros((hist, LANES), F32)
        hc[...] = jnp.zeros_like(hc)

    xbuf[hist:hist + tt, :] = xr_ref[...]
    width = cw_ref.shape[0]
    conv = cb_ref[...] + xbuf[hist - width + 1:hist - width + 1 + tt, :] * cw_ref[0:1, :]
    for j in range(1, width):
        s = hist - width + 1 + j
        conv = conv + xbuf[s:s + tt, :] * cw_ref[j:j + 1, :]
    xbuf[0:hist, :] = xbuf[tt:tt + hist, :]

    a, u = _rg_gates(conv, wa_ref, ba_ref, wi_ref, bi_ref, lam_ref)

    row = lax.broadcasted_iota(I32, (tt, LANES), 0)
    d = 1
    while d < tt:
        keep = row >= d
        a_sh = jnp.where(keep, pltpu.roll(a, d, 0), 1.0)
        u_sh = jnp.where(keep, pltpu.roll(u, d, 0), 0.0)
        u = a * u_sh + u
        a = a * a_sh
        d *= 2
    h = u + a * hc[...]
    hc[...] = h[tt - 1:tt, :]
    hlast_ref[...] = h[tt - 1:tt, :]
    y_ref[...] = h * jax.nn.gelu(gt_ref[...])


def rglru_prompt(z_rg, conv_w, conv_b, w_a, b_a, w_i, b_i, lam):
    t, c2 = z_rg.shape
    c = c2 // 2
    nb = c // LANES
    tt = _row_tile(t, 1024)
    width = conv_w.shape[0]
    vec = pl.BlockSpec((1, LANES), lambda n, i: (0, n))
    wsp = pl.BlockSpec((1, LANES, LANES), lambda n, i: (n, 0, 0))
    return pl.pallas_call(
        _rglru_prompt_kernel,
        grid=(nb, t // tt),
        in_specs=[pl.BlockSpec((tt, LANES), lambda n, i: (i, n)),
                  pl.BlockSpec((tt, LANES), lambda n, i: (i, nb + n)),
                  pl.BlockSpec((width, LANES), lambda n, i: (0, n)),
                  vec, wsp, vec, wsp, vec, vec],
        out_specs=[pl.BlockSpec((tt, LANES), lambda n, i: (i, n)), vec],
        out_shape=[jax.ShapeDtypeStruct((t, c), F32), jax.ShapeDtypeStruct((1, c), F32)],
        scratch_shapes=[pltpu.VMEM((tt + SUBLANES, LANES), F32), pltpu.VMEM((1, LANES), F32)],
        compiler_params=_params("parallel", "arbitrary"),
        name="rglru_prompt",
    )(z_rg, z_rg, conv_w, conv_b.reshape(1, c), w_a.astype(BF16), b_a.reshape(1, c),
      w_i.astype(BF16), b_i.reshape(1, c), lam.reshape(1, c))


def _rglru_sample_kernel(xr_ref, gt_ref, buf_ref, h0_ref, cw_ref, cb_ref, wa_ref, ba_ref, wi_ref,
                         bi_ref, lam_ref, y_ref, hlast_ref, hist, hc):
    width = cw_ref.shape[0]

    @pl.when(pl.program_id(1) == 0)
    def _():
        hist[...] = buf_ref[...]
        hc[...] = h0_ref[...]

    x = xr_ref[0]
    conv = cb_ref[...] + hist[0] * cw_ref[0:1, :]
    for j in range(1, width - 1):
        conv = conv + hist[j] * cw_ref[j:j + 1, :]
    conv = conv + x * cw_ref[width - 1:width, :]
    for j in range(width - 2):
        hist[j] = hist[j + 1]
    hist[width - 2] = x

    a, u = _rg_gates(conv, wa_ref, ba_ref, wi_ref, bi_ref, lam_ref)
    h = a * hc[...] + u
    hc[...] = h
    hlast_ref[...] = h
    y_ref[0] = h * jax.nn.gelu(gt_ref[0])


def rglru_sample(z_rg, buf, h0, conv_w, conv_b, w_a, b_a, w_i, b_i, lam):
    t, b, c2 = z_rg.shape
    c = c2 // 2
    nb = c // LANES
    width = conv_w.shape[0]
    vec = pl.BlockSpec((1, LANES), lambda n, i: (0, n))
    wsp = pl.BlockSpec((1, LANES, LANES), lambda n, i: (n, 0, 0))
    st = pl.BlockSpec((b, LANES), lambda n, i: (0, n))
    return pl.pallas_call(
        _rglru_sample_kernel,
        grid=(nb, t),
        in_specs=[pl.BlockSpec((1, b, LANES), lambda n, i: (i, 0, n)),
                  pl.BlockSpec((1, b, LANES), lambda n, i: (i, 0, nb + n)),
                  pl.BlockSpec((width - 1, b, LANES), lambda n, i: (0, 0, n)),
                  st,
                  pl.BlockSpec((width, LANES), lambda n, i: (0, n)),
                  vec, wsp, vec, wsp, vec, vec],
        out_specs=[pl.BlockSpec((1, b, LANES), lambda n, i: (i, 0, n)), st],
        out_shape=[jax.ShapeDtypeStruct((t, b, c), F32), jax.ShapeDtypeStruct((b, c), F32)],
        scratch_shapes=[pltpu.VMEM((width - 1, b, LANES), F32), pltpu.VMEM((b, LANES), F32)],
        compiler_params=_params("parallel", "arbitrary"),
        name="rglru_sample",
    )(z_rg, z_rg, buf, h0, conv_w, conv_b.reshape(1, c), w_a.astype(BF16), b_a.reshape(1, c),
      w_i.astype(BF16), b_i.reshape(1, c), lam.reshape(1, c))


def _t5_bucket_np(dist, n_buckets):
    max_exact = n_buckets // 2
    d = np.maximum(dist, 1).astype(np.float32)
    large = max_exact + (np.log(d / np.float32(max_exact)) / np.float32(math.log(MAX_DISTANCE / max_exact))
                         * np.float32(n_buckets - max_exact)).astype(np.int32)
    large = np.minimum(large, n_buckets - 1)
    return np.where(dist < max_exact, dist, large).astype(np.int32)


def _bias_kernel(rb_ref, bucket_ref, o_ref, *, n_buckets, far_bucket, scale):
    h = pl.program_id(0)
    bucket = bucket_ref[...]
    shift = rb_ref[far_bucket, h] if far_bucket is not None else 0.0
    acc = jnp.zeros(bucket.shape, F32)
    for b in range(n_buckets):
        acc = jnp.where(bucket == b, (rb_ref[b, h] - shift) * scale, acc)
    o_ref[0] = acc


def bias_tiles(rel_bias, bucket, far_bucket, scale):
    n_buckets, n_heads = rel_bias.shape
    r, c = bucket.shape
    return pl.pallas_call(
        functools.partial(_bias_kernel, n_buckets=n_buckets, far_bucket=far_bucket, scale=scale),
        grid=(n_heads,),
        in_specs=[pl.BlockSpec(memory_space=pltpu.SMEM),
                  pl.BlockSpec((r, c), lambda h: (0, 0))],
        out_specs=pl.BlockSpec((1, r, c), lambda h: (h, 0, 0)),
        out_shape=jax.ShapeDtypeStruct((n_heads, r, c), F32),
        compiler_params=_params("parallel"),
        name="bias_tiles",
    )(rel_bias, jnp.asarray(bucket))


TQ = 128
TK = 256
TK_FULL = 512
LOG2E = math.log2(math.e)
NEAR_TILE_OFFSETS = (-TQ, 0, -TK)


def _pattn_kernel(qi_ref, wit_ref, q_ref, kidx_ref, k_ref, vt_ref, bias_ref, o_ref,
                  keys_s, qis_s, qs_s, s0_s, s1_s, acc_s, m_s, l_s, *, seq_len, n_sel, gqa):
    i = pl.program_id(0)
    n_idx = qis_s.shape[0] // TQ

    for h in range(n_idx):
        qis_s[h * TQ:(h + 1) * TQ, :] = qi_ref[:, h * D_IDX:(h + 1) * D_IDX]
    for kv in range(N_KV):
        for g in range(gqa):
            hh = kv * gqa + g
            qs_s[kv, g * TQ:(g + 1) * TQ, :] = q_ref[:, hh * HEAD_DIM:(hh + 1) * HEAD_DIM]

    qpos = i * TQ + lax.broadcasted_iota(I32, (1, TQ), 1)

    fulls_per_tile = TK_FULL // TQ
    n_full = i // fulls_per_tile
    n_part = (i % fulls_per_tile) // (TK // TQ) + 1
    part0 = n_full * TK_FULL

    def score_tile(k0, tk, masked):
        kt = kidx_ref[pl.ds(k0, tk), :]
        acc = jnp.zeros((tk, TQ), F32)
        for hp in range(n_idx // 2):
            s = lax.dot_general(kt, qis_s[hp * 2 * TQ:(hp + 1) * 2 * TQ, :], NT_DIMS,
                                preferred_element_type=F32)
            for e in range(2):
                h = 2 * hp + e
                acc = acc + jnp.maximum(s[:, e * TQ:(e + 1) * TQ], 0.0) * wit_ref[h:h + 1, :]
        if masked:
            kpos = k0 + lax.broadcasted_iota(I32, (tk, 1), 0)
            acc = jnp.where(kpos <= qpos, acc, NEG)
        keys_s[pl.ds(k0, tk), :] = _sortable_keys(acc)

    def full_scores(j, c):
        score_tile(pl.multiple_of(j * TK_FULL, TK_FULL), TK_FULL, False)
        return c

    def part_scores(j, c):
        score_tile(pl.multiple_of(part0 + j * TK, TK), TK, True)
        return c

    lax.fori_loop(0, n_full, full_scores, 0)
    lax.fori_loop(0, n_part, part_scores, 0)

    n_after = seq_len - (part0 + n_part * TK)

    def count_ge(k0, rows, cand, cnt):
        for c in range(rows // TQ):
            u = keys_s[pl.ds(k0 + c * TQ, TQ), :]
            ge = u.reshape(TQ // SUBLANES, SUBLANES, TQ) >= cand[None]
            cnt = cnt + jnp.sum(jnp.where(ge, 1, 0), axis=0)
        return cnt

    def bit_step(b, thr):
        cand = thr ^ lax.shift_left(jnp.int32(1), 31 - b)
        cnt = lax.fori_loop(
            0, n_full,
            lambda j, cnt: count_ge(pl.multiple_of(j * TK_FULL, TK_FULL), TK_FULL, cand, cnt),
            jnp.zeros((SUBLANES, TQ), I32))
        cnt = lax.fori_loop(
            0, n_part,
            lambda j, cnt: count_ge(pl.multiple_of(part0 + j * TK, TK), TK, cand, cnt), cnt)
        tot = jnp.sum(cnt, axis=0, keepdims=True)
        tot = tot + jnp.where(cand[0:1] <= NEG_KEY, n_after, 0)
        return jnp.where(tot >= n_sel, cand, thr)

    thr = lax.fori_loop(0, 32, bit_step, jnp.full((SUBLANES, TQ), INT_MIN, I32))
    thr = jnp.maximum(thr[0:1], INT_MIN + 1)

    m_s[...] = jnp.full(m_s.shape, NEG, F32)
    l_s[...] = jnp.zeros(l_s.shape, F32)
    acc_s[...] = jnp.zeros(acc_s.shape, F32)

    def logits_stage(j, s_ref, near_kind):
        k0 = j * TK if isinstance(j, int) else pl.multiple_of(j * TK, TK)
        u = keys_s[pl.ds(k0, TK), :]
        if near_kind is not None:
            kpos = k0 + lax.broadcasted_iota(I32, (TK, 1), 0)
            u = jnp.where(kpos <= qpos, u, INT_MIN)
        mask = jnp.where(u >= thr, 0.0, -jnp.inf)
        mask = jnp.concatenate([mask] * gqa, axis=1)
        for kv in range(N_KV):
            kt = k_ref[pl.ds(k0, TK), kv * HEAD_DIM:(kv + 1) * HEAD_DIM]
            s = lax.dot_general(kt, qs_s[kv], NT_DIMS, preferred_element_type=F32)
            if near_kind is not None:
                s = s + bias_ref[near_kind, kv]
            s_ref[kv] = s + mask

    def softmax_stage(j, s_ref):
        k0 = j * TK if isinstance(j, int) else pl.multiple_of(j * TK, TK)
        for kv in range(N_KV):
            m_old = m_s[kv]
            m_new = jnp.maximum(m_old, jnp.max(s_ref[kv], axis=0, keepdims=True))
            alpha = jnp.exp2(m_old - m_new)
            p = jnp.exp2(s_ref[kv] - m_new)
            l_s[kv] = alpha * l_s[kv] + jnp.sum(p, axis=0, keepdims=True)
            vt = vt_ref[kv * HEAD_DIM:(kv + 1) * HEAD_DIM, pl.ds(k0, TK)]
            pv = jnp.dot(vt, p.astype(BF16), preferred_element_type=F32)
            acc_s[kv] = acc_s[kv] * alpha + pv
            m_s[kv] = m_new

    n_tiles = (i + 2) // 2
    n_far = jnp.maximum(i - 1, 0) // 2
    last_far = jnp.maximum(n_far - 1, 0)
    logits_stage(0, s0_s, None)

    def far_pair(p, c):
        logits_stage(2 * p + 1, s1_s, None)
        softmax_stage(2 * p, s0_s)
        logits_stage(jnp.minimum(2 * p + 2, last_far), s0_s, None)
        softmax_stage(2 * p + 1, s1_s)
        return c

    lax.fori_loop(0, n_far // 2, far_pair, 0)

    @pl.when(n_far % 2 == 1)
    def _():
        softmax_stage(n_far - 1, s0_s)

    def near_tile(j, c):
        kind = jnp.where(i % 2 == 1, 0, n_tiles - j)
        logits_stage(j, s0_s, kind)
        softmax_stage(j, s0_s)
        return c

    lax.fori_loop(n_far, n_tiles, near_tile, 0)

    for kv in range(N_KV):
        ot = acc_s[kv] * (1.0 / l_s[kv])
        for g in range(gqa):
            hh = kv * gqa + g
            o_ref[:, hh * HEAD_DIM:(hh + 1) * HEAD_DIM] = ot[:, g * TQ:(g + 1) * TQ].T


def prompt_attention(q, qi, wi_t, kidx, k, v_t, bias, n_sel):
    t, dq = q.shape
    n_heads = dq // HEAD_DIM
    gqa = n_heads // N_KV
    n_idx = qi.shape[1] // D_IDX
    dkv = N_KV * HEAD_DIM
    whole = lambda shape: pl.BlockSpec(shape, lambda i: (0,) * len(shape))
    return pl.pallas_call(
        functools.partial(_pattn_kernel, seq_len=t, n_sel=n_sel, gqa=gqa),
        grid=(t // TQ,),
        in_specs=[pl.BlockSpec((TQ, n_idx * D_IDX), lambda i: (i, 0)),
                  pl.BlockSpec((n_idx, TQ), lambda i: (0, i)),
                  pl.BlockSpec((TQ, dq), lambda i: (i, 0)),
                  whole((t, D_IDX)), whole((t, dkv)), whole((dkv, t)),
                  whole(bias.shape)],
        out_specs=pl.BlockSpec((TQ, dq), lambda i: (i, 0)),
        out_shape=jax.ShapeDtypeStruct((t, dq), F32),
        scratch_shapes=[pltpu.VMEM((t, TQ), I32),
                        pltpu.VMEM((n_idx * TQ, D_IDX), BF16),
                        pltpu.VMEM((N_KV, gqa * TQ, HEAD_DIM), BF16),
                        pltpu.VMEM((N_KV, TK, gqa * TQ), F32),
                        pltpu.VMEM((N_KV, TK, gqa * TQ), F32),
                        pltpu.VMEM((N_KV, HEAD_DIM, gqa * TQ), F32),
                        pltpu.VMEM((N_KV, 1, gqa * TQ), F32),
                        pltpu.VMEM((N_KV, 1, gqa * TQ), F32)],
        compiler_params=_params("parallel"),
        name="prompt_attention",
    )(qi, wi_t, q, kidx, k, v_t, bias)


def _sattn_kernel(pt_ref, qi_ref, wi_ref, q_ref, kin_ref, kn_ref, vn_ref, bias_ref, *rest,
                  n_pages, page, n_sel, gqa):
    ki_pages = rest[:n_pages]
    k_pages = rest[n_pages:2 * n_pages]
    v_pages = rest[2 * n_pages:3 * n_pages]
    o_ref, kibuf, kbuf, vbuf = rest[3 * n_pages:]
    del pt_ref
    past = n_pages * page
    lp = kibuf.shape[0]
    t_new = o_ref.shape[1]
    n_new = kin_ref.shape[1]
    rows = q_ref.shape[1]
    dkv = N_KV * HEAD_DIM

    for p in range(n_pages):
        kibuf[p * page:(p + 1) * page, :] = ki_pages[p][0].astype(BF16)
        for kv in range(N_KV):
            rows_kv = pl.ds(kv, page, stride=N_KV)
            cols_kv = slice(kv * HEAD_DIM, (kv + 1) * HEAD_DIM)
            kbuf[p * page:(p + 1) * page, cols_kv] = k_pages[p][0, rows_kv, :].astype(BF16)
            vbuf[p * page:(p + 1) * page, cols_kv] = v_pages[p][0, rows_kv, :].astype(BF16)
    kibuf[past:, :] = jnp.zeros((lp - past, D_IDX), BF16)
    kbuf[past:, :] = jnp.zeros((lp - past, dkv), BF16)
    vbuf[past:, :] = jnp.zeros((lp - past, dkv), BF16)
    kibuf[past:past + n_new, :] = kin_ref[0]
    kbuf[past:past + n_new, :] = kn_ref[0]
    vbuf[past:past + n_new, :] = vn_ref[0]

    kpos = lax.broadcasted_iota(I32, (1, lp), 1)
    qpos = past + lax.broadcasted_iota(I32, (t_new, 1), 0)
    visible = kpos <= qpos

    s = lax.dot_general(qi_ref[0], kibuf[...], NT_DIMS, preferred_element_type=F32)
    s = jnp.maximum(s, 0.0) * wi_ref[0]
    n_idx = s.shape[0] // t_new
    score = jnp.sum(s.reshape(n_idx, t_new, lp), axis=0)
    score = jnp.where(visible, score, NEG)
    keys = jnp.where(kpos < past + t_new, _sortable_keys(score), INT_MIN)

    def bit_step(b, thr):
        cand = thr ^ lax.shift_left(jnp.int32(1), 31 - b)
        tot = jnp.sum(jnp.where(keys >= cand, 1, 0), axis=1, keepdims=True)
        return jnp.where(tot >= n_sel, cand, thr)

    thr = lax.fori_loop(0, 32, bit_step, jnp.full((t_new, 1), INT_MIN, I32))
    thr = jnp.maximum(thr, INT_MIN + 1)
    sel = jnp.where(visible, keys, INT_MIN) >= thr

    qrep = jnp.concatenate([q_ref[0]] * N_KV, axis=1)
    rkv = lax.broadcasted_iota(I32, (rows, dkv), 0) // (gqa * t_new)
    ckv = lax.broadcasted_iota(I32, (rows, dkv), 1) // HEAD_DIM
    qblk = jnp.where(rkv == ckv, qrep, jnp.zeros_like(qrep))
    logits = lax.dot_general(qblk, kbuf[...], NT_DIMS, preferred_element_type=F32) + bias_ref[...]
    n_heads = rows // t_new
    logits = jnp.where(sel[None], logits.reshape(n_heads, t_new, lp), -jnp.inf)
    m = jnp.maximum(jnp.max(logits, axis=-1, keepdims=True), NEG)
    p = jnp.exp2(logits - m)
    l = jnp.sum(p, axis=-1, keepdims=True)
    o = jnp.dot(p.reshape(rows, lp).astype(BF16), vbuf[...], preferred_element_type=F32)
    o = o.reshape(n_heads, t_new, dkv) * (1.0 / l)
    for kv in range(N_KV):
        for g in range(gqa):
            hh = kv * gqa + g
            o_ref[0, :, hh * HEAD_DIM:(hh + 1) * HEAD_DIM] = o[hh, :, kv * HEAD_DIM:(kv + 1) * HEAD_DIM]


def sample_attention(page_table, qi, wi, q, ki_new, k_new, v_new, bias, cache_ki, cache_k, cache_v,
                     t_new, n_sel):
    b, rows, _ = q.shape
    n_heads = rows // t_new
    gqa = n_heads // N_KV
    n_pages = page_table.shape[1]
    page = cache_ki.shape[1]
    dkv = N_KV * HEAD_DIM
    lp = bias.shape[1]
    n_new = ki_new.shape[1]
    per_seq = lambda shape: pl.BlockSpec((1,) + shape, lambda s, pt: (s, 0, 0))

    def page_spec(page_rows, p):
        return pl.BlockSpec((1, page_rows, HEAD_DIM), lambda s, pt: (pt[s, p], 0, 0))

    in_specs = [per_seq((qi.shape[1], D_IDX)), per_seq((wi.shape[1], 1)), per_seq((rows, HEAD_DIM)),
                per_seq((n_new, D_IDX)), per_seq((n_new, dkv)), per_seq((n_new, dkv)),
                pl.BlockSpec((rows, lp), lambda s, pt: (0, 0))]
    in_specs += [page_spec(page, p) for p in range(n_pages)]
    in_specs += [page_spec(page * N_KV, p) for p in range(n_pages)]
    in_specs += [page_spec(page * N_KV, p) for p in range(n_pages)]
    grid_spec = pltpu.PrefetchScalarGridSpec(
        num_scalar_prefetch=1,
        grid=(b,),
        in_specs=in_specs,
        out_specs=pl.BlockSpec((1, t_new, n_heads * HEAD_DIM), lambda s, pt: (s, 0, 0)),
        scratch_shapes=[pltpu.VMEM((lp, D_IDX), BF16), pltpu.VMEM((lp, dkv), BF16),
                        pltpu.VMEM((lp, dkv), BF16)],
    )
    return pl.pallas_call(
        functools.partial(_sattn_kernel, n_pages=n_pages, page=page, n_sel=n_sel, gqa=gqa),
        grid_spec=grid_spec,
        out_shape=jax.ShapeDtypeStruct((b, t_new, n_heads * HEAD_DIM), F32),
        compiler_params=_params("parallel"),
        name="sample_attention",
    )(page_table, qi, wi, q, ki_new, k_new, v_new, bias,
      *([cache_ki] * n_pages), *([cache_k] * n_pages), *([cache_v] * n_pages))


def _pad_cols(w, n):
    return jnp.pad(w, ((0, 0), (0, n - w.shape[1])))


def _prompt_bias(rel_bias, gqa):
    n_buckets = rel_bias.shape[0]
    n_kinds = len(NEAR_TILE_OFFSETS)
    kl = np.arange(TK)[:, None]
    ql = np.arange(TQ)[None, :]
    dist = np.stack([np.maximum(ql - (off + kl), 0) for off in NEAR_TILE_OFFSETS])
    bucket = _t5_bucket_np(dist, n_buckets).reshape(n_kinds * TK, TQ)
    far = int(_t5_bucket_np(np.array([MAX_DISTANCE + 1]), n_buckets)[0])
    tiles = bias_tiles(rel_bias, bucket, far, LOG2E)
    tiles = tiles.reshape(N_KV, gqa, n_kinds, TK, TQ).transpose(2, 0, 3, 1, 4)
    return tiles.reshape(n_kinds, N_KV, TK, gqa * TQ)


def _sample_bias(rel_bias, past, t_new, lp):
    n_buckets, n_heads = rel_bias.shape
    dist = np.maximum(past + np.arange(t_new)[:, None] - np.arange(lp)[None, :], 0)
    tiles = bias_tiles(rel_bias, _t5_bucket_np(dist, n_buckets), None, LOG2E)
    return tiles.reshape(n_heads * t_new, lp)


def _mixer_in(x, mods, g_pre_mix, w, ln_g, ln_b, sizes):
    d_rnn, d_q, d_kv, d_qi = sizes
    h1 = prenorm(x, g_pre_mix, mods[1], mods[0])
    z_rg = matmul(h1, w["rg"], F32, name="in_proj_rg")
    q = matmul(h1, w["q"], BF16, scale=HEAD_DIM ** -0.5 * LOG2E, name="in_proj_q")
    kv = matmul(h1, w["kv"], F32, name="in_proj_kv")
    qi = matmul(h1, w["qi"], BF16, name="in_proj_qi")
    n_idx = d_qi // D_IDX
    ki, wi = indexer_key_proj(h1, w["kiwi"], ln_g, ln_b, float(d_qi) ** -0.5)
    return z_rg, q, kv[:, :d_kv], kv[:, d_kv:], qi, ki, wi[:, :n_idx]


def _block_out(x, y_rnn, y_attn, mods, w, g_out_rnn, g_out_attn, g_post_mix, g_pre_ffn, g_post_ffn):
    ycat = group_norm_concat(y_rnn, y_attn, g_out_rnn, g_out_attn)
    mixed = matmul(ycat, w["out"], F32, name="out_proj")
    x1, h2 = mid_block(x, mixed, g_post_mix, mods[2], g_pre_ffn, mods[4], mods[3])
    hf = swiglu(h2, w["ffn_gate"], w["ffn_up"])
    f = matmul_ksplit(hf, w["ffn_down"], k_steps=2, name="ffn_down")
    return final_block(x1, f, g_post_ffn, mods[5])


def kernel(x_prompt, x_sample, cache_k, cache_v, cache_idx_k, state_conv, state_h, page_table,
           c_prompt, c_sample, rel_bias, w_mod, b_mod, g_pre_mix, g_post_mix, g_pre_ffn, g_post_ffn,
           w_in, conv_w, conv_b, w_rg_a, b_rg_a, w_rg_i, b_rg_i, lru_lambda, idx_k_ln_g, idx_k_ln_b,
           g_out_rnn, g_out_attn, w_out, w_ffn_gate, w_ffn_up, w_ffn_down):
    depth = w_in.shape[0]
    assert depth == 1, "one layer: prompt/sample caches are not threaded through deeper stacks"
    bp, t_p, d = x_prompt.shape
    assert bp == 1, "the prompt group is one sequence"
    bs, t_s, _ = x_sample.shape
    d_rnn = w_rg_a.shape[1] * w_rg_a.shape[2]
    d_attn = d - d_rnn
    n_heads = d_attn // HEAD_DIM
    gqa = n_heads // N_KV
    d_kv = N_KV * HEAD_DIM
    d_qi = N_IDX_HEADS * D_IDX
    n_pages = page_table.shape[1]
    page = cache_k.shape[2]
    past = n_pages * page
    width = conv_w.shape[1]
    d_ff = w_ffn_gate.shape[2]
    d_ff_pad = _round_up(d_ff, 1024)
    sizes = (d_rnn, d_attn, d_kv, d_qi)

    w_in0 = w_in[0]
    c0 = 2 * d_rnn
    c1 = c0 + d_attn
    c2 = c1 + 2 * d_kv
    c3 = c2 + d_qi
    w = {
        "rg": w_in0[:, :c0].astype(BF16),
        "q": w_in0[:, c0:c1].astype(BF16),
        "kv": w_in0[:, c1:c2].astype(BF16),
        "qi": w_in0[:, c2:c3].astype(BF16),
        "kiwi": _pad_cols(w_in0[:, c3:], D_IDX + LANES).astype(BF16),
        "out": w_out[0].astype(BF16),
        "ffn_gate": _pad_cols(w_ffn_gate[0], d_ff_pad).astype(BF16),
        "ffn_up": _pad_cols(w_ffn_up[0], d_ff_pad).astype(BF16),
        "ffn_down": jnp.pad(w_ffn_down[0], ((0, d_ff_pad - d_ff), (0, 0))).astype(BF16),
    }

    n_c = _round_up(bs + 1, BF16_ROWS)
    c_all = jnp.concatenate([c_sample, c_prompt, jnp.zeros((n_c - bs - 1, d), F32)], axis=0)
    mod_all = modulation(c_all, w_mod[0], b_mod[0])
    mods_p = [mod_all[bs:bs + 1, j * d:(j + 1) * d] for j in range(6)]
    mods_s = [jnp.repeat(mod_all[:bs, j * d:(j + 1) * d], t_s, axis=0) for j in range(6)]

    rnn_w = (conv_w[0], conv_b[0], w_rg_a[0], b_rg_a[0], w_rg_i[0], b_rg_i[0], lru_lambda[0])
    out_w = (g_out_rnn[0], g_out_attn[0], g_post_mix[0], g_pre_ffn[0], g_post_ffn[0])

    xp = x_prompt.reshape(t_p, d)
    z_rg, q, k, v, qi, ki, wi = _mixer_in(xp, mods_p, g_pre_mix[0], w, idx_k_ln_g[0], idx_k_ln_b[0], sizes)
    y_rnn, h_p = rglru_prompt(z_rg, *rnn_w)
    n_sel_p = min(TOPK_MAX, t_p // 4)
    y_attn = prompt_attention(q, qi, wi.T, ki.astype(BF16), k.astype(BF16), v.astype(BF16).T,
                              _prompt_bias(rel_bias, gqa), n_sel_p)
    y_p = _block_out(xp, y_rnn, y_attn, mods_p, w, *out_w)
    k_prompt = k.reshape(1, 1, t_p, N_KV, HEAD_DIM)
    v_prompt = v.reshape(1, 1, t_p, N_KV, HEAD_DIM)
    kidx_prompt = ki.reshape(1, 1, t_p, D_IDX)
    conv_prompt = z_rg[t_p - (width - 1):, :d_rnn].reshape(1, 1, width - 1, d_rnn)
    h_prompt = h_p.reshape(1, 1, d_rnn)

    m_s = bs * t_s
    xs = x_sample.reshape(m_s, d)
    z_rg, q, k, v, qi, ki, wi = _mixer_in(xs, mods_s, g_pre_mix[0], w, idx_k_ln_g[0], idx_k_ln_b[0], sizes)
    z_t = z_rg.reshape(bs, t_s, 2 * d_rnn).transpose(1, 0, 2)
    y_t, h_s = rglru_sample(z_t, state_conv[0].transpose(1, 0, 2), state_h[0], *rnn_w)
    y_rnn = y_t.transpose(1, 0, 2).reshape(m_s, d_rnn)

    def head_major(a, n):
        return a.reshape(bs, t_s, n, a.shape[1] // n).transpose(0, 2, 1, 3).reshape(bs, n * t_s, -1)

    n_new = _round_up(t_s, BF16_ROWS)

    def new_rows(a):
        a = a.reshape(bs, t_s, -1).astype(BF16)
        return jnp.pad(a, ((0, 0), (0, n_new - t_s), (0, 0)))

    lp = past + _round_up(t_s, LANES)
    n_sel_s = min(TOPK_MAX, (past + t_s) // 4)
    wi_rows = wi.reshape(bs, t_s, N_IDX_HEADS).transpose(0, 2, 1).reshape(bs, N_IDX_HEADS * t_s, 1)
    y_attn = sample_attention(
        page_table, head_major(qi, N_IDX_HEADS), wi_rows, head_major(q, n_heads),
        new_rows(ki), new_rows(k), new_rows(v), _sample_bias(rel_bias, past, t_s, lp),
        cache_idx_k[0], cache_k[0].reshape(-1, page * N_KV, HEAD_DIM),
        cache_v[0].reshape(-1, page * N_KV, HEAD_DIM),
        t_s, n_sel_s).reshape(m_s, d_attn)
    y_s = _block_out(xs, y_rnn, y_attn, mods_s, w, *out_w)
    k_sample = k.reshape(1, bs, t_s, N_KV, HEAD_DIM)
    v_sample = v.reshape(1, bs, t_s, N_KV, HEAD_DIM)
    kidx_sample = ki.reshape(1, bs, t_s, D_IDX)
    xr_s = z_rg[:, :d_rnn].reshape(bs, t_s, d_rnn)
    hist = jnp.concatenate([state_conv[0], xr_s], axis=1)[:, t_s:]
    conv_sample = hist.reshape(1, bs, width - 1, d_rnn)
    h_sample = h_s.reshape(1, bs, d_rnn)

    return (y_p.reshape(1, t_p, d), y_s.reshape(bs, t_s, d), k_prompt, v_prompt, kidx_prompt,
            conv_prompt, h_prompt, k_sample, v_sample, kidx_sample, conv_sample, h_sample)
```

```python
import functools
import math

import jax
import jax.numpy as jnp
import numpy as np
from jax import lax
from jax.experimental import pallas as pl
from jax.experimental.pallas import tpu as pltpu

HEAD_DIM = 128
N_KV = 4
N_IDX_HEADS = 16
D_IDX = 128
TOPK_MAX = 256
MAX_DISTANCE = 128
LRU_C = 8.0
EPS = 1e-6
NEG = -1e30

LANES = 128
SUBLANES = 8
BF16_ROWS = 16
VMEM_LIMIT_BYTES = 56 * 1024 * 1024
ROWWISE_TILE = 128

F32 = jnp.float32
BF16 = jnp.bfloat16
I32 = jnp.int32
INT_MIN = -(2 ** 31)
NT_DIMS = (((1,), (1,)), ((), ()))


def _sortable_key_of(x):
    b = int(np.array(x, np.float32).view(np.int32))
    return b ^ ((b >> 31) & 0x7FFFFFFF)


NEG_KEY = _sortable_key_of(NEG)


def _params(*semantics):
    return pltpu.CompilerParams(dimension_semantics=semantics, vmem_limit_bytes=VMEM_LIMIT_BYTES)


def _round_up(x, m):
    return -(-x // m) * m


def _row_tile(m, cap):
    t = min(m, cap)
    while m % t:
        t //= 2
    return t


def _sortable_keys(x):
    bits = pltpu.bitcast(x, I32)
    return bits ^ ((bits >> 31) & 0x7FFFFFFF)


def _rms(x, g):
    return x * lax.rsqrt(jnp.mean(x * x, axis=-1, keepdims=True) + EPS) * g


def _mod_kernel(c_ref, w_ref, b_ref, o_ref, a_s):
    @pl.when(pl.program_id(0) == 0)
    def _():
        c = c_ref[...]
        a_s[...] = (c * jax.nn.sigmoid(c)).astype(BF16)

    acc = jnp.dot(a_s[...], w_ref[...].astype(BF16), preferred_element_type=F32)
    o_ref[...] = acc + b_ref[...]


def modulation(c, w_mod, b_mod):
    b, d = c.shape
    n = w_mod.shape[1]
    tn = 512
    return pl.pallas_call(
        _mod_kernel,
        grid=(n // tn,),
        in_specs=[pl.BlockSpec((b, d), lambda j: (0, 0)),
                  pl.BlockSpec((d, tn), lambda j: (0, j)),
                  pl.BlockSpec((1, tn), lambda j: (0, j))],
        out_specs=pl.BlockSpec((b, tn), lambda j: (0, j)),
        out_shape=jax.ShapeDtypeStruct((b, n), F32),
        scratch_shapes=[pltpu.VMEM((b, d), BF16)],
        compiler_params=_params("arbitrary"),
        name="modulation",
    )(c, w_mod, b_mod.reshape(1, n))


def _mod_spec(mod, tm, d):
    if mod.shape[0] == 1:
        return pl.BlockSpec((1, d), lambda i: (0, 0))
    return pl.BlockSpec((tm, d), lambda i: (i, 0))


def _prenorm_kernel(x_ref, g_ref, sc_ref, sh_ref, o_ref):
    y = _rms(x_ref[...], g_ref[...])
    o_ref[...] = (y * (1.0 + sc_ref[...]) + sh_ref[...]).astype(o_ref.dtype)


def prenorm(x, g, scale, shift):
    m, d = x.shape
    tm = _row_tile(m, ROWWISE_TILE)
    return pl.pallas_call(
        _prenorm_kernel,
        grid=(m // tm,),
        in_specs=[pl.BlockSpec((tm, d), lambda i: (i, 0)),
                  pl.BlockSpec((1, d), lambda i: (0, 0)),
                  _mod_spec(scale, tm, d), _mod_spec(shift, tm, d)],
        out_specs=pl.BlockSpec((tm, d), lambda i: (i, 0)),
        out_shape=jax.ShapeDtypeStruct((m, d), BF16),
        compiler_params=_params("parallel"),
        name="prenorm",
    )(x, g.reshape(1, d), scale, shift)


def _group_norm_kernel(a_ref, b_ref, ga_ref, gb_ref, o_ref):
    da = a_ref.shape[1]
    o_ref[:, :da] = _rms(a_ref[...], ga_ref[...]).astype(o_ref.dtype)
    o_ref[:, da:] = _rms(b_ref[...], gb_ref[...]).astype(o_ref.dtype)


def group_norm_concat(a, b, ga, gb):
    m, da = a.shape
    db = b.shape[1]
    tm = _row_tile(m, ROWWISE_TILE)
    return pl.pallas_call(
        _group_norm_kernel,
        grid=(m // tm,),
        in_specs=[pl.BlockSpec((tm, da), lambda i: (i, 0)),
                  pl.BlockSpec((tm, db), lambda i: (i, 0)),
                  pl.BlockSpec((1, da), lambda i: (0, 0)),
                  pl.BlockSpec((1, db), lambda i: (0, 0))],
        out_specs=pl.BlockSpec((tm, da + db), lambda i: (i, 0)),
        out_shape=jax.ShapeDtypeStruct((m, da + db), BF16),
        compiler_params=_params("parallel"),
        name="group_norm_concat",
    )(a, b, ga.reshape(1, da), gb.reshape(1, db))


def _mid_kernel(x_ref, mix_ref, gpost_ref, gate_ref, gpre_ref, sc_ref, sh_ref, x1_ref, h_ref):
    x1 = x_ref[...] + gate_ref[...] * _rms(mix_ref[...], gpost_ref[...])
    x1_ref[...] = x1
    h_ref[...] = (_rms(x1, gpre_ref[...]) * (1.0 + sc_ref[...]) + sh_ref[...]).astype(h_ref.dtype)


def mid_block(x, mixed, g_post, gate1, g_pre, scale2, shift2):
    m, d = x.shape
    tm = _row_tile(m, ROWWISE_TILE)
    row = pl.BlockSpec((tm, d), lambda i: (i, 0))
    vec = pl.BlockSpec((1, d), lambda i: (0, 0))
    return pl.pallas_call(
        _mid_kernel,
        grid=(m // tm,),
        in_specs=[row, row, vec, _mod_spec(gate1, tm, d), vec,
                  _mod_spec(scale2, tm, d), _mod_spec(shift2, tm, d)],
        out_specs=[row, row],
        out_shape=[jax.ShapeDtypeStruct((m, d), F32), jax.ShapeDtypeStruct((m, d), BF16)],
        compiler_params=_params("parallel"),
        name="mid_block",
    )(x, mixed, g_post.reshape(1, d), gate1, g_pre.reshape(1, d), scale2, shift2)


def _final_kernel(x_ref, f_ref, g_ref, gate_ref, o_ref):
    o_ref[...] = x_ref[...] + gate_ref[...] * _rms(f_ref[...], g_ref[...])


def final_block(x1, f, g_post, gate2):
    m, d = x1.shape
    tm = _row_tile(m, ROWWISE_TILE)
    row = pl.BlockSpec((tm, d), lambda i: (i, 0))
    return pl.pallas_call(
        _final_kernel,
        grid=(m // tm,),
        in_specs=[row, row, pl.BlockSpec((1, d), lambda i: (0, 0)), _mod_spec(gate2, tm, d)],
        out_specs=row,
        out_shape=jax.ShapeDtypeStruct((m, d), F32),
        compiler_params=_params("parallel"),
        name="final_block",
    )(x1, f, g_post.reshape(1, d), gate2)


def _mm_w32_kernel(a_ref, w_ref, o_ref, wbf_s, *, scale):
    @pl.when(pl.program_id(1) == 0)
    def _():
        wbf_s[...] = w_ref[...].astype(BF16)

    acc = jnp.dot(a_ref[...], wbf_s[...], preferred_element_type=F32)
    if scale != 1.0:
        acc = acc * scale
    o_ref[...] = acc.astype(o_ref.dtype)


def matmul_w32(a, w, col0, n, out_dtype, *, scale=1.0, name="matmul_w32"):
    m, k = a.shape
    tm = _row_tile(m, 1024)
    tn = _row_tile(n, 512)
    assert col0 % tn == 0, (col0, tn)
    j0 = col0 // tn
    return pl.pallas_call(
        functools.partial(_mm_w32_kernel, scale=scale),
        grid=(n // tn, m // tm),
        in_specs=[pl.BlockSpec((tm, k), lambda j, i: (i, 0)),
                  pl.BlockSpec((k, tn), lambda j, i: (0, j0 + j))],
        out_specs=pl.BlockSpec((tm, tn), lambda j, i: (i, j)),
        out_shape=jax.ShapeDtypeStruct((m, n), out_dtype),
        scratch_shapes=[pltpu.VMEM((k, tn), BF16)],
        compiler_params=_params("parallel", "arbitrary"),
        name=name,
    )(a, w)


def _mm_acc_kernel(a_ref, w_ref, o_ref, acc_s):
    kk = pl.program_id(2)

    @pl.when(kk == 0)
    def _():
        acc_s[...] = jnp.zeros_like(acc_s)

    acc_s[...] += jnp.dot(a_ref[...], w_ref[...], preferred_element_type=F32)

    @pl.when(kk == pl.num_programs(2) - 1)
    def _():
        o_ref[...] = acc_s[...]


def matmul_ksplit(a, w, *, k_steps, name="matmul_ksplit"):
    m, k = a.shape
    n = w.shape[1]
    tm = _row_tile(m, 1024)
    tn = _row_tile(n, 512)
    tk = k // k_steps
    return pl.pallas_call(
        _mm_acc_kernel,
        grid=(m // tm, n // tn, k_steps),
        in_specs=[pl.BlockSpec((tm, tk), lambda i, j, s: (i, s)),
                  pl.BlockSpec((tk, tn), lambda i, j, s: (s, j))],
        out_specs=pl.BlockSpec((tm, tn), lambda i, j, s: (i, j)),
        out_shape=jax.ShapeDtypeStruct((m, n), F32),
        scratch_shapes=[pltpu.VMEM((tm, tn), F32)],
        compiler_params=_params("parallel", "arbitrary", "arbitrary"),
        name=name,
    )(a, w)


def _swiglu_kernel(a_ref, wg_ref, wu_ref, o_ref):
    a = a_ref[...]
    g = jnp.dot(a, wg_ref[...], preferred_element_type=F32)
    u = jnp.dot(a, wu_ref[...], preferred_element_type=F32)
    o_ref[...] = (g * jax.nn.sigmoid(g) * u).astype(o_ref.dtype)


def swiglu(a, wg, wu):
    m, k = a.shape
    n = wg.shape[1]
    tm = _row_tile(m, 1024)
    tn = _row_tile(n, 512)
    wspec = pl.BlockSpec((k, tn), lambda i, j: (0, j))
    return pl.pallas_call(
        _swiglu_kernel,
        grid=(m // tm, n // tn),
        in_specs=[pl.BlockSpec((tm, k), lambda i, j: (i, 0)), wspec, wspec],
        out_specs=pl.BlockSpec((tm, tn), lambda i, j: (i, j)),
        out_shape=jax.ShapeDtypeStruct((m, n), BF16),
        compiler_params=_params("parallel", "arbitrary"),
        name="swiglu",
    )(a, wg, wu)


def _idxkey_kernel(a_ref, w_ref, g_ref, b_ref, ki_ref, wi_ref, *, wi_scale):
    z = jnp.dot(a_ref[...], w_ref[...], preferred_element_type=F32)
    ki = z[:, :D_IDX]
    mu = jnp.mean(ki, axis=-1, keepdims=True)
    var = jnp.mean(jnp.square(ki - mu), axis=-1, keepdims=True)
    ki_ref[...] = (ki - mu) * lax.rsqrt(var + EPS) * g_ref[...] + b_ref[...]
    wi_ref[...] = z[:, D_IDX:] * wi_scale


def indexer_key_proj(a, w, ln_g, ln_b, wi_scale):
    m, k = a.shape
    n = w.shape[1]
    tm = _row_tile(m, 512)
    return pl.pallas_call(
        functools.partial(_idxkey_kernel, wi_scale=wi_scale),
        grid=(m // tm,),
        in_specs=[pl.BlockSpec((tm, k), lambda i: (i, 0)),
                  pl.BlockSpec((k, n), lambda i: (0, 0)),
                  pl.BlockSpec((1, D_IDX), lambda i: (0, 0)),
                  pl.BlockSpec((1, D_IDX), lambda i: (0, 0))],
        out_specs=[pl.BlockSpec((tm, D_IDX), lambda i: (i, 0)),
                   pl.BlockSpec((tm, n - D_IDX), lambda i: (i, 0))],
        out_shape=[jax.ShapeDtypeStruct((m, D_IDX), F32),
                   jax.ShapeDtypeStruct((m, n - D_IDX), F32)],
        compiler_params=_params("parallel"),
        name="indexer_key_proj",
    )(a, w, ln_g.reshape(1, D_IDX), ln_b.reshape(1, D_IDX))


def _rg_gates(conv, wa_ref, ba_ref, wi_ref, bi_ref, lam_ref):
    cb = conv.astype(BF16)
    r = jax.nn.sigmoid(jnp.dot(cb, wa_ref[0], preferred_element_type=F32) + ba_ref[...])
    i = jax.nn.sigmoid(jnp.dot(cb, wi_ref[0], preferred_element_type=F32) + bi_ref[...])
    nl = -lam_ref[...]
    softplus = jnp.maximum(nl, 0.0) + jnp.log1p(jnp.exp(-jnp.abs(nl)))
    log_a = -LRU_C * r * softplus
    a = jnp.exp(log_a)
    u = jnp.sqrt(-jnp.tanh(log_a) * (a * a + 1.0)) * (i * conv)
    return a, u


def _rglru_prompt_kernel(xr_ref, gt_ref, cw_ref, cb_ref, wa_ref, ba_ref, wi_ref, bi_ref, lam_ref,
                         y_ref, hlast_ref, xbuf, hc):
    tt = xr_ref.shape[0]
    hist = SUBLANES

    @pl.when(pl.program_id(1) == 0)
    def _():
        xbuf[0:hist, :] = jnp.zeros((hist, LANES), F32)
        hc[...] = jnp.zeros_like(hc)

    xbuf[hist:hist + tt, :] = xr_ref[...]
    width = cw_ref.shape[0]
    conv = cb_ref[...] + xbuf[hist - width + 1:hist - width + 1 + tt, :] * cw_ref[0:1, :]
    for j in range(1, width):
        s = hist - width + 1 + j
        conv = conv + xbuf[s:s + tt, :] * cw_ref[j:j + 1, :]
    xbuf[0:hist, :] = xbuf[tt:tt + hist, :]

    a, u = _rg_gates(conv, wa_ref, ba_ref, wi_ref, bi_ref, lam_ref)

    row = lax.broadcasted_iota(I32, (tt, LANES), 0)
    d = 1
    while d < tt:
        keep = row >= d
        a_sh = jnp.where(keep, pltpu.roll(a, d, 0), 1.0)
        u_sh = jnp.where(keep, pltpu.roll(u, d, 0), 0.0)
        u = a * u_sh + u
        a = a * a_sh
        d *= 2
    h = u + a * hc[...]
    hc[...] = h[tt - 1:tt, :]
    hlast_ref[...] = h[tt - 1:tt, :]
    y_ref[...] = h * jax.nn.gelu(gt_ref[...])


def rglru_prompt(z_rg, conv_w, conv_b, w_a, b_a, w_i, b_i, lam):
    t, c2 = z_rg.shape
    c = c2 // 2
    nb = c // LANES
    tt = _row_tile(t, 1024)
    width = conv_w.shape[0]
    vec = pl.BlockSpec((1, LANES), lambda n, i: (0, n))
    wsp = pl.BlockSpec((1, LANES, LANES), lambda n, i: (n, 0, 0))
    return pl.pallas_call(
        _rglru_prompt_kernel,
        grid=(nb, t // tt),
        in_specs=[pl.BlockSpec((tt, LANES), lambda n, i: (i, n)),
                  pl.BlockSpec((tt, LANES), lambda n, i: (i, nb + n)),
                  pl.BlockSpec((width, LANES), lambda n, i: (0, n)),
                  vec, wsp, vec, wsp, vec, vec],
        out_specs=[pl.BlockSpec((tt, LANES), lambda n, i: (i, n)), vec],
        out_shape=[jax.ShapeDtypeStruct((t, c), F32), jax.ShapeDtypeStruct((1, c), F32)],
        scratch_shapes=[pltpu.VMEM((tt + SUBLANES, LANES), F32), pltpu.VMEM((1, LANES), F32)],
        compiler_params=_params("parallel", "arbitrary"),
        name="rglru_prompt",
    )(z_rg, z_rg, conv_w, conv_b.reshape(1, c), w_a.astype(BF16), b_a.reshape(1, c),
      w_i.astype(BF16), b_i.reshape(1, c), lam.reshape(1, c))


def _rglru_sample_kernel(xr_ref, gt_ref, buf_ref, h0_ref, cw_ref, cb_ref, wa_ref, ba_ref, wi_ref,
                         bi_ref, lam_ref, y_ref, hlast_ref, hist, hc):
    width = cw_ref.shape[0]

    @pl.when(pl.program_id(1) == 0)
    def _():
        hist[...] = buf_ref[...]
        hc[...] = h0_ref[...]

    x = xr_ref[0]
    conv = cb_ref[...] + hist[0] * cw_ref[0:1, :]
    for j in range(1, width - 1):
        conv = conv + hist[j] * cw_ref[j:j + 1, :]
    conv = conv + x * cw_ref[width - 1:width, :]
    for j in range(width - 2):
        hist[j] = hist[j + 1]
    hist[width - 2] = x

    a, u = _rg_gates(conv, wa_ref, ba_ref, wi_ref, bi_ref, lam_ref)
    h = a * hc[...] + u
    hc[...] = h
    hlast_ref[...] = h
    y_ref[0] = h * jax.nn.gelu(gt_ref[0])


def rglru_sample(z_rg, buf, h0, conv_w, conv_b, w_a, b_a, w_i, b_i, lam):
    t, b, c2 = z_rg.shape
    c = c2 // 2
    nb = c // LANES
    width = conv_w.shape[0]
    vec = pl.BlockSpec((1, LANES), lambda n, i: (0, n))
    wsp = pl.BlockSpec((1, LANES, LANES), lambda n, i: (n, 0, 0))
    st = pl.BlockSpec((b, LANES), lambda n, i: (0, n))
    return pl.pallas_call(
        _rglru_sample_kernel,
        grid=(nb, t),
        in_specs=[pl.BlockSpec((1, b, LANES), lambda n, i: (i, 0, n)),
                  pl.BlockSpec((1, b, LANES), lambda n, i: (i, 0, nb + n)),
                  pl.BlockSpec((width - 1, b, LANES), lambda n, i: (0, 0, n)),
                  st,
                  pl.BlockSpec((width, LANES), lambda n, i: (0, n)),
                  vec, wsp, vec, wsp, vec, vec],
        out_specs=[pl.BlockSpec((1, b, LANES), lambda n, i: (i, 0, n)), st],
        out_shape=[jax.ShapeDtypeStruct((t, b, c), F32), jax.ShapeDtypeStruct((b, c), F32)],
        scratch_shapes=[pltpu.VMEM((width - 1, b, LANES), F32), pltpu.VMEM((b, LANES), F32)],
        compiler_params=_params("parallel", "arbitrary"),
        name="rglru_sample",
    )(z_rg, z_rg, buf, h0, conv_w, conv_b.reshape(1, c), w_a.astype(BF16), b_a.reshape(1, c),
      w_i.astype(BF16), b_i.reshape(1, c), lam.reshape(1, c))


def _t5_bucket_np(dist, n_buckets):
    max_exact = n_buckets // 2
    d = np.maximum(dist, 1).astype(np.float32)
    large = max_exact + (np.log(d / np.float32(max_exact)) / np.float32(math.log(MAX_DISTANCE / max_exact))
                         * np.float32(n_buckets - max_exact)).astype(np.int32)
    large = np.minimum(large, n_buckets - 1)
    return np.where(dist < max_exact, dist, large).astype(np.int32)


def _bias_kernel(rb_ref, bucket_ref, o_ref, *, n_buckets, far_bucket, scale):
    h = pl.program_id(0)
    bucket = bucket_ref[...]
    shift = rb_ref[far_bucket, h] if far_bucket is not None else 0.0
    acc = jnp.zeros(bucket.shape, F32)
    for b in range(n_buckets):
        acc = jnp.where(bucket == b, (rb_ref[b, h] - shift) * scale, acc)
    o_ref[0] = acc


def bias_tiles(rel_bias, bucket, far_bucket, scale):
    n_buckets, n_heads = rel_bias.shape
    r, c = bucket.shape
    return pl.pallas_call(
        functools.partial(_bias_kernel, n_buckets=n_buckets, far_bucket=far_bucket, scale=scale),
        grid=(n_heads,),
        in_specs=[pl.BlockSpec(memory_space=pltpu.SMEM),
                  pl.BlockSpec((r, c), lambda h: (0, 0))],
        out_specs=pl.BlockSpec((1, r, c), lambda h: (h, 0, 0)),
        out_shape=jax.ShapeDtypeStruct((n_heads, r, c), F32),
        compiler_params=_params("parallel"),
        name="bias_tiles",
    )(rel_bias, jnp.asarray(bucket))


TQ = 128
TK = 256
TK_FULL = 512
LOG2E = math.log2(math.e)
VT_ROWS = HEAD_DIM + BF16_ROWS
NEAR_TILE_OFFSETS = (-TQ, 0, -TK)


def _pattn_kernel(qi_ref, wit_ref, q_ref, kidx_ref, k_ref, vt_ref, bias_ref, o_ref,
                  keys_s, qis_s, qs_s, s0_s, s1_s, acc_s, m_s, *, seq_len, n_sel, gqa):
    i = pl.program_id(0)
    n_idx = qis_s.shape[0] // TQ

    for h in range(n_idx):
        qis_s[h * TQ:(h + 1) * TQ, :] = qi_ref[:, h * D_IDX:(h + 1) * D_IDX]
    for kv in range(N_KV):
        for g in range(gqa):
            hh = kv * gqa + g
            qs_s[kv, g * TQ:(g + 1) * TQ, :] = q_ref[:, hh * HEAD_DIM:(hh + 1) * HEAD_DIM]

    qpos = i * TQ + lax.broadcasted_iota(I32, (1, TQ), 1)

    fulls_per_tile = TK_FULL // TQ
    n_full = i // fulls_per_tile
    n_part = (i % fulls_per_tile) // (TK // TQ) + 1
    part0 = n_full * TK_FULL

    def score_tile(k0, tk, masked):
        kt = kidx_ref[pl.ds(k0, tk), :]
        acc = jnp.zeros((tk, TQ), F32)
        for hp in range(n_idx // 2):
            s = lax.dot_general(kt, qis_s[hp * 2 * TQ:(hp + 1) * 2 * TQ, :], NT_DIMS,
                                preferred_element_type=F32)
            for e in range(2):
                h = 2 * hp + e
                acc = acc + jnp.maximum(s[:, e * TQ:(e + 1) * TQ], 0.0) * wit_ref[h:h + 1, :]
        if masked:
            kpos = k0 + lax.broadcasted_iota(I32, (tk, 1), 0)
            acc = jnp.where(kpos <= qpos, acc, NEG)
        keys_s[pl.ds(k0, tk), :] = _sortable_keys(acc)

    def full_scores(j, c):
        score_tile(pl.multiple_of(j * TK_FULL, TK_FULL), TK_FULL, False)
        return c

    def part_scores(j, c):
        score_tile(pl.multiple_of(part0 + j * TK, TK), TK, True)
        return c

    lax.fori_loop(0, n_full, full_scores, 0)
    lax.fori_loop(0, n_part, part_scores, 0)

    n_after = seq_len - (part0 + n_part * TK)

    def count_ge(k0, rows, cand, cnt):
        for c in range(rows // TQ):
            u = keys_s[pl.ds(k0 + c * TQ, TQ), :]
            ge = u.reshape(TQ // SUBLANES, SUBLANES, TQ) >= cand[None]
            cnt = cnt + jnp.sum(jnp.where(ge, 1, 0), axis=0)
        return cnt

    def bit_step(b, thr):
        cand = thr ^ lax.shift_left(jnp.int32(1), 31 - b)
        cnt = lax.fori_loop(
            0, n_full,
            lambda j, cnt: count_ge(pl.multiple_of(j * TK_FULL, TK_FULL), TK_FULL, cand, cnt),
            jnp.zeros((SUBLANES, TQ), I32))
        cnt = lax.fori_loop(
            0, n_part,
            lambda j, cnt: count_ge(pl.multiple_of(part0 + j * TK, TK), TK, cand, cnt), cnt)
        tot = jnp.sum(cnt, axis=0, keepdims=True)
        tot = tot + jnp.where(cand[0:1] <= NEG_KEY, n_after, 0)
        return jnp.where(tot >= n_sel, cand, thr)

    thr = lax.fori_loop(0, 32, bit_step, jnp.full((SUBLANES, TQ), INT_MIN, I32))
    thr = jnp.maximum(thr[0:1], INT_MIN + 1)

    m_s[...] = jnp.full(m_s.shape, NEG, F32)
    acc_s[...] = jnp.zeros(acc_s.shape, F32)

    def logits_stage(j, s_ref, near_kind):
        k0 = j * TK if isinstance(j, int) else pl.multiple_of(j * TK, TK)
        u = keys_s[pl.ds(k0, TK), :]
        if near_kind is not None:
            kpos = k0 + lax.broadcasted_iota(I32, (TK, 1), 0)
            u = jnp.where(kpos <= qpos, u, INT_MIN)
        mask = jnp.where(u >= thr, 0.0, -jnp.inf)
        mask = jnp.concatenate([mask] * gqa, axis=1)
        for kv in range(N_KV):
            kt = k_ref[pl.ds(k0, TK), kv * HEAD_DIM:(kv + 1) * HEAD_DIM]
            s = lax.dot_general(kt, qs_s[kv], NT_DIMS, preferred_element_type=F32)
            if near_kind is not None:
                s = s + bias_ref[near_kind, kv]
            s_ref[kv] = s + mask

    def softmax_stage(j, s_ref):
        k0 = j * TK if isinstance(j, int) else pl.multiple_of(j * TK, TK)
        for kv in range(N_KV):
            m_old = m_s[kv]
            m_new = jnp.maximum(m_old, jnp.max(s_ref[kv], axis=0, keepdims=True))
            alpha = jnp.exp2(m_old - m_new)
            p = jnp.exp2(s_ref[kv] - m_new)
            vt = vt_ref[kv * VT_ROWS:(kv + 1) * VT_ROWS, pl.ds(k0, TK)]
            pv = jnp.dot(vt, p.astype(BF16), preferred_element_type=F32)
            acc_s[kv] = acc_s[kv] * alpha + pv
            m_s[kv] = m_new

    n_tiles = (i + 2) // 2
    n_far = jnp.maximum(i - 1, 0) // 2
    last_far = jnp.maximum(n_far - 1, 0)
    logits_stage(0, s0_s, None)

    def far_pair(p, c):
        logits_stage(2 * p + 1, s1_s, None)
        softmax_stage(2 * p, s0_s)
        logits_stage(jnp.minimum(2 * p + 2, last_far), s0_s, None)
        softmax_stage(2 * p + 1, s1_s)
        return c

    lax.fori_loop(0, n_far // 2, far_pair, 0)

    @pl.when(n_far % 2 == 1)
    def _():
        softmax_stage(n_far - 1, s0_s)

    def near_tile(j, c):
        kind = jnp.where(i % 2 == 1, 0, n_tiles - j)
        logits_stage(j, s0_s, kind)
        softmax_stage(j, s0_s)
        return c

    lax.fori_loop(n_far, n_tiles, near_tile, 0)

    for kv in range(N_KV):
        ot = acc_s[kv, :HEAD_DIM, :] * (1.0 / acc_s[kv, HEAD_DIM:HEAD_DIM + 1, :])
        for g in range(gqa):
            hh = kv * gqa + g
            o_ref[:, hh * HEAD_DIM:(hh + 1) * HEAD_DIM] = ot[:, g * TQ:(g + 1) * TQ].T


def prompt_attention(q, qi, wi_t, kidx, k, v_t, bias, n_sel):
    t, dq = q.shape
    n_heads = dq // HEAD_DIM
    gqa = n_heads // N_KV
    n_idx = qi.shape[1] // D_IDX
    dkv = N_KV * HEAD_DIM
    whole = lambda shape: pl.BlockSpec(shape, lambda i: (0,) * len(shape))
    return pl.pallas_call(
        functools.partial(_pattn_kernel, seq_len=t, n_sel=n_sel, gqa=gqa),
        grid=(t // TQ,),
        in_specs=[pl.BlockSpec((TQ, n_idx * D_IDX), lambda i: (i, 0)),
                  pl.BlockSpec((n_idx, TQ), lambda i: (0, i)),
                  pl.BlockSpec((TQ, dq), lambda i: (i, 0)),
                  whole((t, D_IDX)), whole((t, dkv)), whole((N_KV * VT_ROWS, t)),
                  whole(bias.shape)],
        out_specs=pl.BlockSpec((TQ, dq), lambda i: (i, 0)),
        out_shape=jax.ShapeDtypeStruct((t, dq), F32),
        scratch_shapes=[pltpu.VMEM((t, TQ), I32),
                        pltpu.VMEM((n_idx * TQ, D_IDX), BF16),
                        pltpu.VMEM((N_KV, gqa * TQ, HEAD_DIM), BF16),
                        pltpu.VMEM((N_KV, TK, gqa * TQ), F32),
                        pltpu.VMEM((N_KV, TK, gqa * TQ), F32),
                        pltpu.VMEM((N_KV, VT_ROWS, gqa * TQ), F32),
                        pltpu.VMEM((N_KV, 1, gqa * TQ), F32)],
        compiler_params=_params("parallel"),
        name="prompt_attention",
    )(qi, wi_t, q, kidx, k, v_t, bias)


RADIX_BITS = 4

def _sattn_kernel(pt_ref, qi_ref, wi_ref, q_ref, kin_ref, kn_ref, vn_ref, bias_ref, *rest,
                  n_pages, page, n_sel, gqa):
    ki_pages = rest[:n_pages]
    k_pages = rest[n_pages:2 * n_pages]
    v_pages = rest[2 * n_pages:3 * n_pages]
    o_ref, kibuf, kbuf, vbuf = rest[3 * n_pages:]
    del pt_ref
    past = n_pages * page
    lp = kibuf.shape[0]
    t_new = o_ref.shape[1]
    n_new = kin_ref.shape[1]
    rows = q_ref.shape[1]
    dkv = N_KV * HEAD_DIM

    for p in range(n_pages):
        kibuf[p * page:(p + 1) * page, :] = ki_pages[p][0].astype(BF16)
        for kv in range(N_KV):
            rows_kv = pl.ds(kv, page, stride=N_KV)
            cols_kv = slice(kv * HEAD_DIM, (kv + 1) * HEAD_DIM)
            kbuf[p * page:(p + 1) * page, cols_kv] = k_pages[p][0, rows_kv, :].astype(BF16)
            vbuf[p * page:(p + 1) * page, cols_kv] = v_pages[p][0, rows_kv, :].astype(BF16)
    kibuf[past:, :] = jnp.zeros((lp - past, D_IDX), BF16)
    kbuf[past:, :] = jnp.zeros((lp - past, dkv), BF16)
    vbuf[past:, :] = jnp.zeros((lp - past, dkv), BF16)
    kibuf[past:past + n_new, :] = kin_ref[0]
    kbuf[past:past + n_new, :] = kn_ref[0]
    vbuf[past:past + n_new, :] = vn_ref[0]

    kpos = lax.broadcasted_iota(I32, (1, lp), 1)
    qpos = past + lax.broadcasted_iota(I32, (t_new, 1), 0)
    visible = kpos <= qpos

    s = lax.dot_general(qi_ref[0], kibuf[...], NT_DIMS, preferred_element_type=F32)
    s = jnp.maximum(s, 0.0) * wi_ref[0]
    n_idx = s.shape[0] // t_new
    score = jnp.sum(s.reshape(n_idx, t_new, lp), axis=0)
    score = jnp.where(visible, score, NEG)
    keys = jnp.where(kpos < past + t_new, _sortable_keys(score), INT_MIN)

    prefix = jnp.zeros((t_new, 1), I32)
    for shift in range(32 - RADIX_BITS, -1, -RADIX_BITS):
        digit = jnp.zeros((t_new, 1), I32)
        for v in range(1, 2 ** RADIX_BITS):
            step = int(np.array((v << shift) & 0xFFFFFFFF, np.uint32).view(np.int32))
            cand = (prefix | step) ^ INT_MIN
            tot = jnp.sum(jnp.where(keys >= cand, 1, 0), axis=1, keepdims=True)
            digit = digit + jnp.where(tot >= n_sel, 1, 0)
        prefix = prefix | lax.shift_left(digit, jnp.int32(shift))
    thr = jnp.maximum(prefix ^ INT_MIN, INT_MIN + 1)
    sel = jnp.where(visible, keys, INT_MIN) >= thr

    qrep = jnp.concatenate([q_ref[0]] * N_KV, axis=1)
    rkv = lax.broadcasted_iota(I32, (rows, dkv), 0) // (gqa * t_new)
    ckv = lax.broadcasted_iota(I32, (rows, dkv), 1) // HEAD_DIM
    qblk = jnp.where(rkv == ckv, qrep, jnp.zeros_like(qrep))
    logits = lax.dot_general(qblk, kbuf[...], NT_DIMS, preferred_element_type=F32) + bias_ref[...]
    n_heads = rows // t_new
    logits = jnp.where(sel[None], logits.reshape(n_heads, t_new, lp), -jnp.inf)
    m = jnp.maximum(jnp.max(logits, axis=-1, keepdims=True), NEG)
    p = jnp.exp2(logits - m)
    l = jnp.sum(p, axis=-1, keepdims=True)
    o = jnp.dot(p.reshape(rows, lp).astype(BF16), vbuf[...], preferred_element_type=F32)
    o = o.reshape(n_heads, t_new, dkv) * (1.0 / l)
    for kv in range(N_KV):
        for g in range(gqa):
            hh = kv * gqa + g
            o_ref[0, :, hh * HEAD_DIM:(hh + 1) * HEAD_DIM] = o[hh, :, kv * HEAD_DIM:(kv + 1) * HEAD_DIM]


def sample_attention(page_table, qi, wi, q, ki_new, k_new, v_new, bias, cache_ki, cache_k, cache_v,
                     t_new, n_sel):
    b, rows, _ = q.shape
    n_heads = rows // t_new
    gqa = n_heads // N_KV
    n_pages = page_table.shape[1]
    page = cache_ki.shape[1]
    dkv = N_KV * HEAD_DIM
    lp = bias.shape[1]
    n_new = ki_new.shape[1]
    per_seq = lambda shape: pl.BlockSpec((1,) + shape, lambda s, pt: (s, 0, 0))

    def page_spec(page_rows, p):
        return pl.BlockSpec((1, page_rows, HEAD_DIM), lambda s, pt: (pt[s, p], 0, 0))

    in_specs = [per_seq((qi.shape[1], D_IDX)), per_seq((wi.shape[1], 1)), per_seq((rows, HEAD_DIM)),
                per_seq((n_new, D_IDX)), per_seq((n_new, dkv)), per_seq((n_new, dkv)),
                pl.BlockSpec((rows, lp), lambda s, pt: (0, 0))]
    in_specs += [page_spec(page, p) for p in range(n_pages)]
    in_specs += [page_spec(page * N_KV, p) for p in range(n_pages)]
    in_specs += [page_spec(page * N_KV, p) for p in range(n_pages)]
    grid_spec = pltpu.PrefetchScalarGridSpec(
        num_scalar_prefetch=1,
        grid=(b,),
        in_specs=in_specs,
        out_specs=pl.BlockSpec((1, t_new, n_heads * HEAD_DIM), lambda s, pt: (s, 0, 0)),
        scratch_shapes=[pltpu.VMEM((lp, D_IDX), BF16), pltpu.VMEM((lp, dkv), BF16),
                        pltpu.VMEM((lp, dkv), BF16)],
    )
    return pl.pallas_call(
        functools.partial(_sattn_kernel, n_pages=n_pages, page=page, n_sel=n_sel, gqa=gqa),
        grid_spec=grid_spec,
        out_shape=jax.ShapeDtypeStruct((b, t_new, n_heads * HEAD_DIM), F32),
        compiler_params=_params("parallel"),
        name="sample_attention",
    )(page_table, qi, wi, q, ki_new, k_new, v_new, bias,
      *([cache_ki] * n_pages), *([cache_k] * n_pages), *([cache_v] * n_pages))


def _pad_cols(w, n):
    return jnp.pad(w, ((0, 0), (0, n - w.shape[1])))


def _prompt_bias(rel_bias, gqa):
    n_buckets = rel_bias.shape[0]
    n_kinds = len(NEAR_TILE_OFFSETS)
    kl = np.arange(TK)[:, None]
    ql = np.arange(TQ)[None, :]
    dist = np.stack([np.maximum(ql - (off + kl), 0) for off in NEAR_TILE_OFFSETS])
    bucket = _t5_bucket_np(dist, n_buckets).reshape(n_kinds * TK, TQ)
    far = int(_t5_bucket_np(np.array([MAX_DISTANCE + 1]), n_buckets)[0])
    tiles = bias_tiles(rel_bias, bucket, far, LOG2E)
    tiles = tiles.reshape(N_KV, gqa, n_kinds, TK, TQ).transpose(2, 0, 3, 1, 4)
    return tiles.reshape(n_kinds, N_KV, TK, gqa * TQ)


def _sample_bias(rel_bias, past, t_new, lp):
    n_buckets, n_heads = rel_bias.shape
    dist = np.maximum(past + np.arange(t_new)[:, None] - np.arange(lp)[None, :], 0)
    tiles = bias_tiles(rel_bias, _t5_bucket_np(dist, n_buckets), None, LOG2E)
    return tiles.reshape(n_heads * t_new, lp)


def _mixer_in(x, mods, g_pre_mix, w, ln_g, ln_b, sizes):
    d_rnn, d_q, d_kv, d_qi = sizes
    h1 = prenorm(x, g_pre_mix, mods[1], mods[0])
    c_q = 2 * d_rnn
    c_kv = c_q + d_q
    c_qi = c_kv + 2 * d_kv
    z_rg = matmul_w32(h1, w["in"], 0, c_q, F32, name="in_proj_rg")
    q = matmul_w32(h1, w["in"], c_q, d_q, BF16, scale=HEAD_DIM ** -0.5 * LOG2E, name="in_proj_q")
    kv = matmul_w32(h1, w["in"], c_kv, 2 * d_kv, F32, name="in_proj_kv")
    qi = matmul_w32(h1, w["in"], c_qi, d_qi, BF16, name="in_proj_qi")
    n_idx = d_qi // D_IDX
    ki, wi = indexer_key_proj(h1, w["kiwi"], ln_g, ln_b, float(d_qi) ** -0.5)
    return z_rg, q, kv[:, :d_kv], kv[:, d_kv:], qi, ki, wi[:, :n_idx]


def _block_out(x, y_rnn, y_attn, mods, w, g_out_rnn, g_out_attn, g_post_mix, g_pre_ffn, g_post_ffn):
    ycat = group_norm_concat(y_rnn, y_attn, g_out_rnn, g_out_attn)
    mixed = matmul_w32(ycat, w["out"], 0, w["out"].shape[1], F32, name="out_proj")
    x1, h2 = mid_block(x, mixed, g_post_mix, mods[2], g_pre_ffn, mods[4], mods[3])
    hf = swiglu(h2, w["ffn_gate"], w["ffn_up"])
    f = matmul_ksplit(hf, w["ffn_down"], k_steps=2, name="ffn_down")
    return final_block(x1, f, g_post_ffn, mods[5])


def kernel(x_prompt, x_sample, cache_k, cache_v, cache_idx_k, state_conv, state_h, page_table,
           c_prompt, c_sample, rel_bias, w_mod, b_mod, g_pre_mix, g_post_mix, g_pre_ffn, g_post_ffn,
           w_in, conv_w, conv_b, w_rg_a, b_rg_a, w_rg_i, b_rg_i, lru_lambda, idx_k_ln_g, idx_k_ln_b,
           g_out_rnn, g_out_attn, w_out, w_ffn_gate, w_ffn_up, w_ffn_down):
    depth = w_in.shape[0]
    assert depth == 1, "one layer: prompt/sample caches are not threaded through deeper stacks"
    bp, t_p, d = x_prompt.shape
    assert bp == 1, "the prompt group is one sequence"
    bs, t_s, _ = x_sample.shape
    d_rnn = w_rg_a.shape[1] * w_rg_a.shape[2]
    d_attn = d - d_rnn
    n_heads = d_attn // HEAD_DIM
    gqa = n_heads // N_KV
    d_kv = N_KV * HEAD_DIM
    d_qi = N_IDX_HEADS * D_IDX
    n_pages = page_table.shape[1]
    page = cache_k.shape[2]
    past = n_pages * page
    width = conv_w.shape[1]
    d_ff = w_ffn_gate.shape[2]
    d_ff_pad = _round_up(d_ff, 1024)
    sizes = (d_rnn, d_attn, d_kv, d_qi)

    w_in0 = w_in[0]
    c3 = 2 * d_rnn + d_attn + 2 * d_kv + d_qi
    w = {
        "in": w_in0,
        "kiwi": _pad_cols(w_in0[:, c3:], D_IDX + LANES).astype(BF16),
        "out": w_out[0],
        "ffn_gate": _pad_cols(w_ffn_gate[0], d_ff_pad).astype(BF16),
        "ffn_up": _pad_cols(w_ffn_up[0], d_ff_pad).astype(BF16),
        "ffn_down": jnp.pad(w_ffn_down[0], ((0, d_ff_pad - d_ff), (0, 0))).astype(BF16),
    }

    n_c = _round_up(bs + 1, BF16_ROWS)
    c_all = jnp.concatenate([c_sample, c_prompt, jnp.zeros((n_c - bs - 1, d), F32)], axis=0)
    mod_all = modulation(c_all, w_mod[0], b_mod[0])
    mods_p = [mod_all[bs:bs + 1, j * d:(j + 1) * d] for j in range(6)]
    mods_s = [jnp.repeat(mod_all[:bs, j * d:(j + 1) * d], t_s, axis=0) for j in range(6)]

    rnn_w = (conv_w[0], conv_b[0], w_rg_a[0], b_rg_a[0], w_rg_i[0], b_rg_i[0], lru_lambda[0])
    out_w = (g_out_rnn[0], g_out_attn[0], g_post_mix[0], g_pre_ffn[0], g_post_ffn[0])

    xp = x_prompt.reshape(t_p, d)
    z_rg, q, k, v, qi, ki, wi = _mixer_in(xp, mods_p, g_pre_mix[0], w, idx_k_ln_g[0], idx_k_ln_b[0], sizes)
    y_rnn, h_p = rglru_prompt(z_rg, *rnn_w)
    n_sel_p = min(TOPK_MAX, t_p // 4)
    v_heads = v.astype(BF16).reshape(t_p, N_KV, HEAD_DIM)
    v_ones = jnp.ones((t_p, N_KV, VT_ROWS - HEAD_DIM), BF16)
    v_t = jnp.concatenate([v_heads, v_ones], axis=2).reshape(t_p, N_KV * VT_ROWS).T
    y_attn = prompt_attention(q, qi, wi.T, ki.astype(BF16), k.astype(BF16), v_t,
                              _prompt_bias(rel_bias, gqa), n_sel_p)
    y_p = _block_out(xp, y_rnn, y_attn, mods_p, w, *out_w)
    k_prompt = k.reshape(1, 1, t_p, N_KV, HEAD_DIM)
    v_prompt = v.reshape(1, 1, t_p, N_KV, HEAD_DIM)
    kidx_prompt = ki.reshape(1, 1, t_p, D_IDX)
    conv_prompt = z_rg[t_p - (width - 1):, :d_rnn].reshape(1, 1, width - 1, d_rnn)
    h_prompt = h_p.reshape(1, 1, d_rnn)

    m_s = bs * t_s
    xs = x_sample.reshape(m_s, d)
    z_rg, q, k, v, qi, ki, wi = _mixer_in(xs, mods_s, g_pre_mix[0], w, idx_k_ln_g[0], idx_k_ln_b[0], sizes)
    z_t = z_rg.reshape(bs, t_s, 2 * d_rnn).transpose(1, 0, 2)
    y_t, h_s = rglru_sample(z_t, state_conv[0].transpose(1, 0, 2), state_h[0], *rnn_w)
    y_rnn = y_t.transpose(1, 0, 2).reshape(m_s, d_rnn)

    def head_major(a, n):
        return a.reshape(bs, t_s, n, a.shape[1] // n).transpose(0, 2, 1, 3).reshape(bs, n * t_s, -1)

    n_new = _round_up(t_s, BF16_ROWS)

    def new_rows(a):
        a = a.reshape(bs, t_s, -1).astype(BF16)
        return jnp.pad(a, ((0, 0), (0, n_new - t_s), (0, 0)))

    lp = past + _round_up(t_s, LANES)
    n_sel_s = min(TOPK_MAX, (past + t_s) // 4)
    wi_rows = wi.reshape(bs, t_s, N_IDX_HEADS).transpose(0, 2, 1).reshape(bs, N_IDX_HEADS * t_s, 1)
    y_attn = sample_attention(
        page_table, head_major(qi, N_IDX_HEADS), wi_rows, head_major(q, n_heads),
        new_rows(ki), new_rows(k), new_rows(v), _sample_bias(rel_bias, past, t_s, lp),
        cache_idx_k[0], cache_k[0].reshape(-1, page * N_KV, HEAD_DIM),
        cache_v[0].reshape(-1, page * N_KV, HEAD_DIM),
        t_s, n_sel_s).reshape(m_s, d_attn)
    y_s = _block_out(xs, y_rnn, y_attn, mods_s, w, *out_w)
    k_sample = k.reshape(1, bs, t_s, N_KV, HEAD_DIM)
    v_sample = v.reshape(1, bs, t_s, N_KV, HEAD_DIM)
    kidx_sample = ki.reshape(1, bs, t_s, D_IDX)
    xr_s = z_rg[:, :d_rnn].reshape(bs, t_s, d_rnn)
    hist = jnp.concatenate([state_conv[0], xr_s], axis=1)[:, t_s:]
    conv_sample = hist.reshape(1, bs, width - 1, d_rnn)
    h_sample = h_s.reshape(1, bs, d_rnn)

    return (y_p.reshape(1, t_p, d), y_s.reshape(bs, t_s, d), k_prompt, v_prompt, kidx_prompt,
            conv_prompt, h_prompt, k_sample, v_sample, kidx_sample, conv_sample, h_sample)
```

```python
import functools
import math

import jax
import jax.numpy as jnp
import numpy as np
from jax import lax
from jax.experimental import pallas as pl
from jax.experimental.pallas import tpu as pltpu

HEAD_DIM = 128
N_KV = 4
N_IDX_HEADS = 16
D_IDX = 128
TOPK_MAX = 256
MAX_DISTANCE = 128
LRU_C = 8.0
EPS = 1e-6
NEG = -1e30

LANES = 128
SUBLANES = 8
BF16_ROWS = 16
VMEM_LIMIT_BYTES = 56 * 1024 * 1024
ROWWISE_TILE = 128

F32 = jnp.float32
BF16 = jnp.bfloat16
I32 = jnp.int32
I16 = jnp.int16
INT_MIN = -(2 ** 31)
INT16_MIN = -(2 ** 15)
NT_DIMS = (((1,), (1,)), ((), ()))


def _sortable_key_of(x):
    b = int(np.array(x, np.float32).view(np.int32))
    return b ^ ((b >> 31) & 0x7FFFFFFF)


NEG_KEY = _sortable_key_of(NEG)
NEG_KEY_HI = NEG_KEY >> 16
NEG_KEY_LO = (NEG_KEY & 0xFFFF) + INT16_MIN


def _params(*semantics):
    return pltpu.CompilerParams(dimension_semantics=semantics, vmem_limit_bytes=VMEM_LIMIT_BYTES)


def _round_up(x, m):
    return -(-x // m) * m


def _row_tile(m, cap):
    t = min(m, cap)
    while m % t:
        t //= 2
    return t


def _sortable_keys(x):
    bits = pltpu.bitcast(x, I32)
    return bits ^ ((bits >> 31) & 0x7FFFFFFF)


def _rms(x, g):
    return x * lax.rsqrt(jnp.mean(x * x, axis=-1, keepdims=True) + EPS) * g


def _mod_kernel(c_ref, w_ref, b_ref, o_ref, a_s):
    @pl.when(pl.program_id(0) == 0)
    def _():
        c = c_ref[...]
        a_s[...] = (c * jax.nn.sigmoid(c)).astype(BF16)

    acc = jnp.dot(a_s[...], w_ref[...].astype(BF16), preferred_element_type=F32)
    o_ref[...] = acc + b_ref[...]


def modulation(c, w_mod, b_mod):
    b, d = c.shape
    n = w_mod.shape[1]
    tn = 512
    return pl.pallas_call(
        _mod_kernel,
        grid=(n // tn,),
        in_specs=[pl.BlockSpec((b, d), lambda j: (0, 0)),
                  pl.BlockSpec((d, tn), lambda j: (0, j)),
                  pl.BlockSpec((1, tn), lambda j: (0, j))],
        out_specs=pl.BlockSpec((b, tn), lambda j: (0, j)),
        out_shape=jax.ShapeDtypeStruct((b, n), F32),
        scratch_shapes=[pltpu.VMEM((b, d), BF16)],
        compiler_params=_params("arbitrary"),
        name="modulation",
    )(c, w_mod, b_mod.reshape(1, n))


def _mod_spec(mod, tm, d):
    if mod.shape[0] == 1:
        return pl.BlockSpec((1, d), lambda i: (0, 0))
    return pl.BlockSpec((tm, d), lambda i: (i, 0))


def _prenorm_kernel(x_ref, g_ref, sc_ref, sh_ref, o_ref):
    y = _rms(x_ref[...], g_ref[...])
    o_ref[...] = (y * (1.0 + sc_ref[...]) + sh_ref[...]).astype(o_ref.dtype)


def prenorm(x, g, scale, shift):
    m, d = x.shape
    tm = _row_tile(m, ROWWISE_TILE)
    return pl.pallas_call(
        _prenorm_kernel,
        grid=(m // tm,),
        in_specs=[pl.BlockSpec((tm, d), lambda i: (i, 0)),
                  pl.BlockSpec((1, d), lambda i: (0, 0)),
                  _mod_spec(scale, tm, d), _mod_spec(shift, tm, d)],
        out_specs=pl.BlockSpec((tm, d), lambda i: (i, 0)),
        out_shape=jax.ShapeDtypeStruct((m, d), BF16),
        compiler_params=_params("parallel"),
        name="prenorm",
    )(x, g.reshape(1, d), scale, shift)


def _group_norm_kernel(a_ref, b_ref, ga_ref, gb_ref, o_ref):
    da = a_ref.shape[1]
    o_ref[:, :da] = _rms(a_ref[...], ga_ref[...]).astype(o_ref.dtype)
    o_ref[:, da:] = _rms(b_ref[...], gb_ref[...]).astype(o_ref.dtype)


def group_norm_concat(a, b, ga, gb):
    m, da = a.shape
    db = b.shape[1]
    tm = _row_tile(m, ROWWISE_TILE)
    return pl.pallas_call(
        _group_norm_kernel,
        grid=(m // tm,),
        in_specs=[pl.BlockSpec((tm, da), lambda i: (i, 0)),
                  pl.BlockSpec((tm, db), lambda i: (i, 0)),
                  pl.BlockSpec((1, da), lambda i: (0, 0)),
                  pl.BlockSpec((1, db), lambda i: (0, 0))],
        out_specs=pl.BlockSpec((tm, da + db), lambda i: (i, 0)),
        out_shape=jax.ShapeDtypeStruct((m, da + db), BF16),
        compiler_params=_params("parallel"),
        name="group_norm_concat",
    )(a, b, ga.reshape(1, da), gb.reshape(1, db))


def _mid_kernel(x_ref, mix_ref, gpost_ref, gate_ref, gpre_ref, sc_ref, sh_ref, x1_ref, h_ref):
    x1 = x_ref[...] + gate_ref[...] * _rms(mix_ref[...], gpost_ref[...])
    x1_ref[...] = x1
    h_ref[...] = (_rms(x1, gpre_ref[...]) * (1.0 + sc_ref[...]) + sh_ref[...]).astype(h_ref.dtype)


def mid_block(x, mixed, g_post, gate1, g_pre, scale2, shift2):
    m, d = x.shape
    tm = _row_tile(m, ROWWISE_TILE)
    row = pl.BlockSpec((tm, d), lambda i: (i, 0))
    vec = pl.BlockSpec((1, d), lambda i: (0, 0))
    return pl.pallas_call(
        _mid_kernel,
        grid=(m // tm,),
        in_specs=[row, row, vec, _mod_spec(gate1, tm, d), vec,
                  _mod_spec(scale2, tm, d), _mod_spec(shift2, tm, d)],
        out_specs=[row, row],
        out_shape=[jax.ShapeDtypeStruct((m, d), F32), jax.ShapeDtypeStruct((m, d), BF16)],
        compiler_params=_params("parallel"),
        name="mid_block",
    )(x, mixed, g_post.reshape(1, d), gate1, g_pre.reshape(1, d), scale2, shift2)


def _final_kernel(x_ref, f_ref, g_ref, gate_ref, o_ref):
    o_ref[...] = x_ref[...] + gate_ref[...] * _rms(f_ref[...], g_ref[...])


def final_block(x1, f, g_post, gate2):
    m, d = x1.shape
    tm = _row_tile(m, ROWWISE_TILE)
    row = pl.BlockSpec((tm, d), lambda i: (i, 0))
    return pl.pallas_call(
        _final_kernel,
        grid=(m // tm,),
        in_specs=[row, row, pl.BlockSpec((1, d), lambda i: (0, 0)), _mod_spec(gate2, tm, d)],
        out_specs=row,
        out_shape=jax.ShapeDtypeStruct((m, d), F32),
        compiler_params=_params("parallel"),
        name="final_block",
    )(x1, f, g_post.reshape(1, d), gate2)


def _mm_w32_kernel(a_ref, w_ref, o_ref, wbf_s, *, scale, w_is_transposed):
    @pl.when(pl.program_id(1) == 0)
    def _():
        wbf_s[...] = w_ref[...].astype(BF16)

    if w_is_transposed:
        acc = lax.dot_general(a_ref[...], wbf_s[...], NT_DIMS, preferred_element_type=F32)
    else:
        acc = jnp.dot(a_ref[...], wbf_s[...], preferred_element_type=F32)
    if scale != 1.0:
        acc = acc * scale
    o_ref[...] = acc.astype(o_ref.dtype)


def matmul_w32(a, w, col0, n, out_dtype, *, scale=1.0, w_is_transposed=False, name="matmul_w32"):
    m, k = a.shape
    tm = _row_tile(m, 1024)
    tn = _row_tile(n, 512)
    assert col0 % tn == 0, (col0, tn)
    j0 = col0 // tn
    if w_is_transposed:
        w_block = (tn, k)
        w_spec = pl.BlockSpec(w_block, lambda j, i: (j0 + j, 0))
    else:
        w_block = (k, tn)
        w_spec = pl.BlockSpec(w_block, lambda j, i: (0, j0 + j))
    return pl.pallas_call(
        functools.partial(_mm_w32_kernel, scale=scale, w_is_transposed=w_is_transposed),
        grid=(n // tn, m // tm),
        in_specs=[pl.BlockSpec((tm, k), lambda j, i: (i, 0)), w_spec],
        out_specs=pl.BlockSpec((tm, tn), lambda j, i: (i, j)),
        out_shape=jax.ShapeDtypeStruct((m, n), out_dtype),
        scratch_shapes=[pltpu.VMEM(w_block, BF16)],
        compiler_params=_params("parallel", "arbitrary"),
        name=name,
    )(a, w)


def _mm_acc_kernel(a_ref, w_ref, o_ref, acc_s):
    kk = pl.program_id(2)

    @pl.when(kk == 0)
    def _():
        acc_s[...] = jnp.zeros_like(acc_s)

    acc_s[...] += jnp.dot(a_ref[...], w_ref[...], preferred_element_type=F32)

    @pl.when(kk == pl.num_programs(2) - 1)
    def _():
        o_ref[...] = acc_s[...]


def matmul_ksplit(a, w, *, k_steps, name="matmul_ksplit"):
    m, k = a.shape
    n = w.shape[1]
    tm = _row_tile(m, 1024)
    tn = _row_tile(n, 512)
    tk = k // k_steps
    return pl.pallas_call(
        _mm_acc_kernel,
        grid=(m // tm, n // tn, k_steps),
        in_specs=[pl.BlockSpec((tm, tk), lambda i, j, s: (i, s)),
                  pl.BlockSpec((tk, tn), lambda i, j, s: (s, j))],
        out_specs=pl.BlockSpec((tm, tn), lambda i, j, s: (i, j)),
        out_shape=jax.ShapeDtypeStruct((m, n), F32),
        scratch_shapes=[pltpu.VMEM((tm, tn), F32)],
        compiler_params=_params("parallel", "arbitrary", "arbitrary"),
        name=name,
    )(a, w)


def _swiglu_kernel(a_ref, wg_ref, wu_ref, o_ref, wg_s, wu_s):
    @pl.when(pl.program_id(1) == 0)
    def _():
        wg_s[...] = wg_ref[...].astype(BF16)
        wu_s[...] = wu_ref[...].astype(BF16)

    a = a_ref[...]
    g = jnp.dot(a, wg_s[...], preferred_element_type=F32)
    u = jnp.dot(a, wu_s[...], preferred_element_type=F32)
    o_ref[...] = (g * jax.nn.sigmoid(g) * u).astype(o_ref.dtype)


def swiglu(a, wg, wu):
    m, k = a.shape
    n = wg.shape[1]
    tm = _row_tile(m, 1024)
    tn = _row_tile(n, 256)
    wspec = pl.BlockSpec((k, tn), lambda j, i: (0, j))
    return pl.pallas_call(
        _swiglu_kernel,
        grid=(n // tn, m // tm),
        in_specs=[pl.BlockSpec((tm, k), lambda j, i: (i, 0)), wspec, wspec],
        out_specs=pl.BlockSpec((tm, tn), lambda j, i: (i, j)),
        out_shape=jax.ShapeDtypeStruct((m, n), BF16),
        scratch_shapes=[pltpu.VMEM((k, tn), BF16), pltpu.VMEM((k, tn), BF16)],
        compiler_params=_params("parallel", "arbitrary"),
        name="swiglu",
    )(a, wg, wu)


def _idxkey_kernel(a_ref, w_ref, g_ref, b_ref, ki_ref, wi_ref, *, wi_scale):
    z = jnp.dot(a_ref[...], w_ref[...], preferred_element_type=F32)
    ki = z[:, :D_IDX]
    mu = jnp.mean(ki, axis=-1, keepdims=True)
    var = jnp.mean(jnp.square(ki - mu), axis=-1, keepdims=True)
    ki_ref[...] = (ki - mu) * lax.rsqrt(var + EPS) * g_ref[...] + b_ref[...]
    wi_ref[...] = z[:, D_IDX:] * wi_scale


def indexer_key_proj(a, w, ln_g, ln_b, wi_scale):
    m, k = a.shape
    n = w.shape[1]
    tm = _row_tile(m, 512)
    return pl.pallas_call(
        functools.partial(_idxkey_kernel, wi_scale=wi_scale),
        grid=(m // tm,),
        in_specs=[pl.BlockSpec((tm, k), lambda i: (i, 0)),
                  pl.BlockSpec((k, n), lambda i: (0, 0)),
                  pl.BlockSpec((1, D_IDX), lambda i: (0, 0)),
                  pl.BlockSpec((1, D_IDX), lambda i: (0, 0))],
        out_specs=[pl.BlockSpec((tm, D_IDX), lambda i: (i, 0)),
                   pl.BlockSpec((tm, n - D_IDX), lambda i: (i, 0))],
        out_shape=[jax.ShapeDtypeStruct((m, D_IDX), F32),
                   jax.ShapeDtypeStruct((m, n - D_IDX), F32)],
        compiler_params=_params("parallel"),
        name="indexer_key_proj",
    )(a, w, ln_g.reshape(1, D_IDX), ln_b.reshape(1, D_IDX))


def _rg_gates(conv, wa_ref, ba_ref, wi_ref, bi_ref, lam_ref):
    cb = conv.astype(BF16)
    r = jax.nn.sigmoid(jnp.dot(cb, wa_ref[0], preferred_element_type=F32) + ba_ref[...])
    i = jax.nn.sigmoid(jnp.dot(cb, wi_ref[0], preferred_element_type=F32) + bi_ref[...])
    nl = -lam_ref[...]
    softplus = jnp.maximum(nl, 0.0) + jnp.log1p(jnp.exp(-jnp.abs(nl)))
    log_a = -LRU_C * r * softplus
    a = jnp.exp(log_a)
    u = jnp.sqrt(-jnp.tanh(log_a) * (a * a + 1.0)) * (i * conv)
    return a, u


def _rglru_prompt_kernel(xr_ref, gt_ref, cw_ref, cb_ref, wa_ref, ba_ref, wi_ref, bi_ref, lam_ref,
                         y_ref, hlast_ref, xbuf, hc):
    tt = xr_ref.shape[0]
    hist = SUBLANES

    @pl.when(pl.program_id(1) == 0)
    def _():
        xbuf[0:hist, :] = jnp.zeros((hist, LANES), F32)
        hc[...] = jnp.zeros_like(hc)

    xbuf[hist:hist + tt, :] = xr_ref[...]
    width = cw_ref.shape[0]
    conv = cb_ref[...] + xbuf[hist - width + 1:hist - width + 1 + tt, :] * cw_ref[0:1, :]
    for j in range(1, width):
        s = hist - width + 1 + j
        conv = conv + xbuf[s:s + tt, :] * cw_ref[j:j + 1, :]
    xbuf[0:hist, :] = xbuf[tt:tt + hist, :]

    a, u = _rg_gates(conv, wa_ref, ba_ref, wi_ref, bi_ref, lam_ref)

    row = lax.broadcasted_iota(I32, (tt, LANES), 0)
    d = 1
    while d < tt:
        keep = row >= d
        a_sh = jnp.where(keep, pltpu.roll(a, d, 0), 1.0)
        u_sh = jnp.where(keep, pltpu.roll(u, d, 0), 0.0)
        u = a * u_sh + u
        a = a * a_sh
        d *= 2
    h = u + a * hc[...]
    hc[...] = h[tt - 1:tt, :]
    hlast_ref[...] = h[tt - 1:tt, :]
    y_ref[...] = h * jax.nn.gelu(gt_ref[...])


def rglru_prompt(z_rg, conv_w, conv_b, w_a, b_a, w_i, b_i, lam):
    t, c2 = z_rg.shape
    c = c2 // 2
    nb = c // LANES
    tt = _row_tile(t, 1024)
    width = conv_w.shape[0]
    vec = pl.BlockSpec((1, LANES), lambda n, i: (0, n))
    wsp = pl.BlockSpec((1, LANES, LANES), lambda n, i: (n, 0, 0))
    return pl.pallas_call(
        _rglru_prompt_kernel,
        grid=(nb, t // tt),
        in_specs=[pl.BlockSpec((tt, LANES), lambda n, i: (i, n)),
                  pl.BlockSpec((tt, LANES), lambda n, i: (i, nb + n)),
                  pl.BlockSpec((width, LANES), lambda n, i: (0, n)),
                  vec, wsp, vec, wsp, vec, vec],
        out_specs=[pl.BlockSpec((tt, LANES), lambda n, i: (i, n)), vec],
        out_shape=[jax.ShapeDtypeStruct((t, c), F32), jax.ShapeDtypeStruct((1, c), F32)],
        scratch_shapes=[pltpu.VMEM((tt + SUBLANES, LANES), F32), pltpu.VMEM((1, LANES), F32)],
        compiler_params=_params("parallel", "arbitrary"),
        name="rglru_prompt",
    )(z_rg, z_rg, conv_w, conv_b.reshape(1, c), w_a.astype(BF16), b_a.reshape(1, c),
      w_i.astype(BF16), b_i.reshape(1, c), lam.reshape(1, c))


def _rglru_sample_kernel(xr_ref, gt_ref, buf_ref, h0_ref, cw_ref, cb_ref, wa_ref, ba_ref, wi_ref,
                         bi_ref, lam_ref, y_ref, hlast_ref, hist, hc):
    width = cw_ref.shape[0]

    @pl.when(pl.program_id(1) == 0)
    def _():
        hist[...] = buf_ref[...]
        hc[...] = h0_ref[...]

    x = xr_ref[0]
    conv = cb_ref[...] + hist[0] * cw_ref[0:1, :]
    for j in range(1, width - 1):
        conv = conv + hist[j] * cw_ref[j:j + 1, :]
    conv = conv + x * cw_ref[width - 1:width, :]
    for j in range(width - 2):
        hist[j] = hist[j + 1]
    hist[width - 2] = x

    a, u = _rg_gates(conv, wa_ref, ba_ref, wi_ref, bi_ref, lam_ref)
    h = a * hc[...] + u
    hc[...] = h
    hlast_ref[...] = h
    y_ref[0] = h * jax.nn.gelu(gt_ref[0])


def rglru_sample(z_rg, buf, h0, conv_w, conv_b, w_a, b_a, w_i, b_i, lam):
    t, b, c2 = z_rg.shape
    c = c2 // 2
    nb = c // LANES
    width = conv_w.shape[0]
    vec = pl.BlockSpec((1, LANES), lambda n, i: (0, n))
    wsp = pl.BlockSpec((1, LANES, LANES), lambda n, i: (n, 0, 0))
    st = pl.BlockSpec((b, LANES), lambda n, i: (0, n))
    return pl.pallas_call(
        _rglru_sample_kernel,
        grid=(nb, t),
        in_specs=[pl.BlockSpec((1, b, LANES), lambda n, i: (i, 0, n)),
                  pl.BlockSpec((1, b, LANES), lambda n, i: (i, 0, nb + n)),
                  pl.BlockSpec((width - 1, b, LANES), lambda n, i: (0, 0, n)),
                  st,
                  pl.BlockSpec((width, LANES), lambda n, i: (0, n)),
                  vec, wsp, vec, wsp, vec, vec],
        out_specs=[pl.BlockSpec((1, b, LANES), lambda n, i: (i, 0, n)), st],
        out_shape=[jax.ShapeDtypeStruct((t, b, c), F32), jax.ShapeDtypeStruct((b, c), F32)],
        scratch_shapes=[pltpu.VMEM((width - 1, b, LANES), F32), pltpu.VMEM((b, LANES), F32)],
        compiler_params=_params("parallel", "arbitrary"),
        name="rglru_sample",
    )(z_rg, z_rg, buf, h0, conv_w, conv_b.reshape(1, c), w_a.astype(BF16), b_a.reshape(1, c),
      w_i.astype(BF16), b_i.reshape(1, c), lam.reshape(1, c))


def _t5_bucket_np(dist, n_buckets):
    max_exact = n_buckets // 2
    d = np.maximum(dist, 1).astype(np.float32)
    large = max_exact + (np.log(d / np.float32(max_exact)) / np.float32(math.log(MAX_DISTANCE / max_exact))
                         * np.float32(n_buckets - max_exact)).astype(np.int32)
    large = np.minimum(large, n_buckets - 1)
    return np.where(dist < max_exact, dist, large).astype(np.int32)


def _bias_kernel(rb_ref, bucket_ref, o_ref, *, n_buckets, far_bucket, scale):
    h = pl.program_id(0)
    bucket = bucket_ref[...]
    shift = rb_ref[far_bucket, h] if far_bucket is not None else 0.0
    acc = jnp.zeros(bucket.shape, F32)
    for b in range(n_buckets):
        acc = jnp.where(bucket == b, (rb_ref[b, h] - shift) * scale, acc)
    o_ref[0] = acc


def bias_tiles(rel_bias, bucket, far_bucket, scale):
    n_buckets, n_heads = rel_bias.shape
    r, c = bucket.shape
    return pl.pallas_call(
        functools.partial(_bias_kernel, n_buckets=n_buckets, far_bucket=far_bucket, scale=scale),
        grid=(n_heads,),
        in_specs=[pl.BlockSpec(memory_space=pltpu.SMEM),
                  pl.BlockSpec((r, c), lambda h: (0, 0))],
        out_specs=pl.BlockSpec((1, r, c), lambda h: (h, 0, 0)),
        out_shape=jax.ShapeDtypeStruct((n_heads, r, c), F32),
        compiler_params=_params("parallel"),
        name="bias_tiles",
    )(rel_bias, jnp.asarray(bucket))


TQ = 128
TK = 256
TK_FULL = 512
LOG2E = math.log2(math.e)
VT_ROWS = HEAD_DIM + BF16_ROWS
NEAR_TILE_OFFSETS = (-TQ, 0, -TK)


def _pattn_kernel(qi_ref, wit_ref, q_ref, kidx_ref, k_ref, vt_ref, bias_ref, o_ref,
                  keys_s, khi_s, klo_s, qis_s, qs_s, s0_s, s1_s, acc_s, m_s, *, seq_len, n_sel, gqa):
    i = pl.program_id(0)
    n_idx = qis_s.shape[0] // TQ

    for h in range(n_idx):
        qis_s[h * TQ:(h + 1) * TQ, :] = qi_ref[:, h * D_IDX:(h + 1) * D_IDX]
    for kv in range(N_KV):
        for g in range(gqa):
            hh = kv * gqa + g
            qs_s[kv, g * TQ:(g + 1) * TQ, :] = q_ref[:, hh * HEAD_DIM:(hh + 1) * HEAD_DIM]

    qpos = i * TQ + lax.broadcasted_iota(I32, (1, TQ), 1)

    fulls_per_tile = TK_FULL // TQ
    n_full = i // fulls_per_tile
    n_part = (i % fulls_per_tile) // (TK // TQ) + 1
    part0 = n_full * TK_FULL

    def score_tile(k0, tk, masked):
        kt = kidx_ref[pl.ds(k0, tk), :]
        acc = jnp.zeros((tk, TQ), F32)
        for hp in range(n_idx // 2):
            s = lax.dot_general(kt, qis_s[hp * 2 * TQ:(hp + 1) * 2 * TQ, :], NT_DIMS,
                                preferred_element_type=F32)
            for e in range(2):
                h = 2 * hp + e
                acc = acc + jnp.maximum(s[:, e * TQ:(e + 1) * TQ], 0.0) * wit_ref[h:h + 1, :]
        if masked:
            kpos = k0 + lax.broadcasted_iota(I32, (tk, 1), 0)
            acc = jnp.where(kpos <= qpos, acc, NEG)
        keys = _sortable_keys(acc)
        keys_s[pl.ds(k0, tk), :] = keys
        khi_s[pl.ds(k0, tk), :] = (keys >> 16).astype(I16)
        klo_s[pl.ds(k0, tk), :] = ((keys & 0xFFFF) + INT16_MIN).astype(I16)

    def full_scores(j, c):
        score_tile(pl.multiple_of(j * TK_FULL, TK_FULL), TK_FULL, False)
        return c

    def part_scores(j, c):
        score_tile(pl.multiple_of(part0 + j * TK, TK), TK, True)
        return c

    lax.fori_loop(0, n_full, full_scores, 0)
    lax.fori_loop(0, n_part, part_scores, 0)

    n_after = seq_len - (part0 + n_part * TK)

    one16 = jnp.ones((BF16_ROWS, TQ), I16)
    zero16 = jnp.zeros((BF16_ROWS, TQ), I16)

    def over_tiles(tile_fn, carry):
        carry = lax.fori_loop(
            0, n_full, lambda j, c: tile_fn(pl.multiple_of(j * TK_FULL, TK_FULL), TK_FULL, c), carry)
        return lax.fori_loop(
            0, n_part, lambda j, c: tile_fn(pl.multiple_of(part0 + j * TK, TK), TK, c), carry)

    def count_where(mask_fn, slab, cnts):
        for c in range(slab.shape[0] // BF16_ROWS):
            hit = jnp.where(mask_fn(slab[c * BF16_ROWS:(c + 1) * BF16_ROWS, :]), one16, zero16)
            cnts = cnts[1:] + (cnts[0] + hit,)
        return cnts

    def total(cnts):
        cnt = cnts[0]
        for c in cnts[1:]:
            cnt = cnt + c
        return jnp.sum(cnt.astype(I32), axis=0, keepdims=True)

    def search16(ref, need, after_fn):
        def bit_step(b, t):
            cand = t + lax.shift_left(jnp.int32(1), 15 - b)
            cand16 = jnp.broadcast_to(cand, (BF16_ROWS, TQ)).astype(I16)

            def tile_fn(k0, rows, cnts):
                return count_where(lambda x: x >= cand16, ref[pl.ds(k0, rows), :], cnts)

            tot = total(over_tiles(tile_fn, (zero16,) * 4)) + after_fn(cand)
            return jnp.where(tot >= need, cand, t)

        return lax.fori_loop(0, 16, bit_step, jnp.full((1, TQ), INT16_MIN, I32))

    t_hi = search16(khi_s, n_sel, lambda cand: jnp.where(cand <= NEG_KEY_HI, n_after, 0))
    t_hi16 = jnp.broadcast_to(t_hi, (BF16_ROWS, TQ)).astype(I16)

    def split_tile(k0, rows, cnts):
        hi = khi_s[pl.ds(k0, rows), :]
        t_slab = jnp.concatenate([t_hi16] * (rows // BF16_ROWS), axis=0)
        klo_s[pl.ds(k0, rows), :] = jnp.where(hi == t_slab, klo_s[pl.ds(k0, rows), :], jnp.int16(INT16_MIN))
        return count_where(lambda x: x > t_hi16, hi, cnts)

    n_above = total(over_tiles(split_tile, (zero16,) * 4)) + jnp.where(NEG_KEY_HI > t_hi, n_after, 0)
    t_lo = search16(
        klo_s, n_sel - n_above,
        lambda cand: jnp.where(t_hi == NEG_KEY_HI, jnp.where(cand <= NEG_KEY_LO, n_after, 0), 0))
    thr = t_hi * 65536 + (t_lo - INT16_MIN)
    thr = jnp.maximum(thr, INT_MIN + 1)

    m_s[...] = jnp.full(m_s.shape, NEG, F32)
    acc_s[...] = jnp.zeros(acc_s.shape, F32)

    def logits_stage(j, s_ref, near_kind):
        k0 = j * TK if isinstance(j, int) else pl.multiple_of(j * TK, TK)
        u = keys_s[pl.ds(k0, TK), :]
        if near_kind is not None:
            kpos = k0 + lax.broadcasted_iota(I32, (TK, 1), 0)
            u = jnp.where(kpos <= qpos, u, INT_MIN)
        mask = jnp.where(u >= thr, 0.0, -jnp.inf)
        mask = jnp.concatenate([mask] * gqa, axis=1)
        for kv in range(N_KV):
            kt = k_ref[pl.ds(k0, TK), kv * HEAD_DIM:(kv + 1) * HEAD_DIM]
            s = lax.dot_general(kt, qs_s[kv], NT_DIMS, preferred_element_type=F32)
            if near_kind is not None:
                s = s + bias_ref[near_kind, kv]
            s_ref[kv] = s + mask

    def softmax_stage(j, s_ref):
        k0 = j * TK if isinstance(j, int) else pl.multiple_of(j * TK, TK)
        for kv in range(N_KV):
            m_old = m_s[kv]
            m_new = jnp.maximum(m_old, jnp.max(s_ref[kv], axis=0, keepdims=True))
            alpha = jnp.exp2(m_old - m_new)
            p = jnp.exp2(s_ref[kv] - m_new)
            vt = vt_ref[kv * VT_ROWS:(kv + 1) * VT_ROWS, pl.ds(k0, TK)]
            pv = jnp.dot(vt, p.astype(BF16), preferred_element_type=F32)
            acc_s[kv] = acc_s[kv] * alpha + pv
            m_s[kv] = m_new

    n_tiles = (i + 2) // 2
    n_far = jnp.maximum(i - 1, 0) // 2
    last_far = jnp.maximum(n_far - 1, 0)
    logits_stage(0, s0_s, None)

    def far_pair(p, c):
        logits_stage(2 * p + 1, s1_s, None)
        softmax_stage(2 * p, s0_s)
        logits_stage(jnp.minimum(2 * p + 2, last_far), s0_s, None)
        softmax_stage(2 * p + 1, s1_s)
        return c

    lax.fori_loop(0, n_far // 2, far_pair, 0)

    @pl.when(n_far % 2 == 1)
    def _():
        softmax_stage(n_far - 1, s0_s)

    def near_tile(j, c):
        kind = jnp.where(i % 2 == 1, 0, n_tiles - j)
        logits_stage(j, s0_s, kind)
        softmax_stage(j, s0_s)
        return c

    lax.fori_loop(n_far, n_tiles, near_tile, 0)

    for kv in range(N_KV):
        ot = acc_s[kv, :HEAD_DIM, :] * (1.0 / acc_s[kv, HEAD_DIM:HEAD_DIM + 1, :])
        for g in range(gqa):
            hh = kv * gqa + g
            o_ref[:, hh * HEAD_DIM:(hh + 1) * HEAD_DIM] = ot[:, g * TQ:(g + 1) * TQ].T


def prompt_attention(q, qi, wi_t, kidx, k, v_t, bias, n_sel):
    t, dq = q.shape
    n_heads = dq // HEAD_DIM
    gqa = n_heads // N_KV
    n_idx = qi.shape[1] // D_IDX
    dkv = N_KV * HEAD_DIM
    whole = lambda shape: pl.BlockSpec(shape, lambda i: (0,) * len(shape))
    return pl.pallas_call(
        functools.partial(_pattn_kernel, seq_len=t, n_sel=n_sel, gqa=gqa),
        grid=(t // TQ,),
        in_specs=[pl.BlockSpec((TQ, n_idx * D_IDX), lambda i: (i, 0)),
                  pl.BlockSpec((n_idx, TQ), lambda i: (0, i)),
                  pl.BlockSpec((TQ, dq), lambda i: (i, 0)),
                  whole((t, D_IDX)), whole((t, dkv)), whole((N_KV * VT_ROWS, t)),
                  whole(bias.shape)],
        out_specs=pl.BlockSpec((TQ, dq), lambda i: (i, 0)),
        out_shape=jax.ShapeDtypeStruct((t, dq), F32),
        scratch_shapes=[pltpu.VMEM((t, TQ), I32),
                        pltpu.VMEM((t, TQ), I16),
                        pltpu.VMEM((t, TQ), I16),
                        pltpu.VMEM((n_idx * TQ, D_IDX), BF16),
                        pltpu.VMEM((N_KV, gqa * TQ, HEAD_DIM), BF16),
                        pltpu.VMEM((N_KV, TK, gqa * TQ), F32),
                        pltpu.VMEM((N_KV, TK, gqa * TQ), F32),
                        pltpu.VMEM((N_KV, VT_ROWS, gqa * TQ), F32),
                        pltpu.VMEM((N_KV, 1, gqa * TQ), F32)],
        compiler_params=_params("parallel"),
        name="prompt_attention",
    )(qi, wi_t, q, kidx, k, v_t, bias)


RADIX_BITS = 4

def _sattn_kernel(pt_ref, qi_ref, wi_ref, q_ref, kin_ref, kn_ref, vn_ref, bias_ref, *rest,
                  n_pages, page, n_sel, gqa):
    ki_pages = rest[:n_pages]
    k_pages = rest[n_pages:2 * n_pages]
    v_pages = rest[2 * n_pages:3 * n_pages]
    o_ref, kibuf, kbuf, vbuf = rest[3 * n_pages:]
    del pt_ref
    past = n_pages * page
    lp = kibuf.shape[0]
    t_new = o_ref.shape[1]
    n_new = kin_ref.shape[1]
    rows = q_ref.shape[1]
    dkv = N_KV * HEAD_DIM

    for p in range(n_pages):
        kibuf[p * page:(p + 1) * page, :] = ki_pages[p][0].astype(BF16)
        for kv in range(N_KV):
            rows_kv = pl.ds(kv, page, stride=N_KV)
            cols_kv = slice(kv * HEAD_DIM, (kv + 1) * HEAD_DIM)
            kbuf[p * page:(p + 1) * page, cols_kv] = k_pages[p][0, rows_kv, :].astype(BF16)
            vbuf[p * page:(p + 1) * page, cols_kv] = v_pages[p][0, rows_kv, :].astype(BF16)
    kibuf[past:, :] = jnp.zeros((lp - past, D_IDX), BF16)
    kbuf[past:, :] = jnp.zeros((lp - past, dkv), BF16)
    vbuf[past:, :] = jnp.zeros((lp - past, dkv), BF16)
    kibuf[past:past + n_new, :] = kin_ref[0]
    kbuf[past:past + n_new, :] = kn_ref[0]
    vbuf[past:past + n_new, :] = vn_ref[0]

    kpos = lax.broadcasted_iota(I32, (1, lp), 1)
    qpos = past + lax.broadcasted_iota(I32, (t_new, 1), 0)
    visible = kpos <= qpos

    s = lax.dot_general(qi_ref[0], kibuf[...], NT_DIMS, preferred_element_type=F32)
    s = jnp.maximum(s, 0.0) * wi_ref[0]
    n_idx = s.shape[0] // t_new
    score = jnp.sum(s.reshape(n_idx, t_new, lp), axis=0)
    score = jnp.where(visible, score, NEG)
    keys = jnp.where(kpos < past + t_new, _sortable_keys(score), INT_MIN)

    prefix = jnp.zeros((t_new, 1), I32)
    for shift in range(32 - RADIX_BITS, -1, -RADIX_BITS):
        digit = jnp.zeros((t_new, 1), I32)
        for v in range(1, 2 ** RADIX_BITS):
            step = int(np.array((v << shift) & 0xFFFFFFFF, np.uint32).view(np.int32))
            cand = (prefix | step) ^ INT_MIN
            tot = jnp.sum(jnp.where(keys >= cand, 1, 0), axis=1, keepdims=True)
            digit = digit + jnp.where(tot >= n_sel, 1, 0)
        prefix = prefix | lax.shift_left(digit, jnp.int32(shift))
    thr = jnp.maximum(prefix ^ INT_MIN, INT_MIN + 1)
    sel = jnp.where(visible, keys, INT_MIN) >= thr

    qrep = jnp.concatenate([q_ref[0]] * N_KV, axis=1)
    rkv = lax.broadcasted_iota(I32, (rows, dkv), 0) // (gqa * t_new)
    ckv = lax.broadcasted_iota(I32, (rows, dkv), 1) // HEAD_DIM
    qblk = jnp.where(rkv == ckv, qrep, jnp.zeros_like(qrep))
    logits = lax.dot_general(qblk, kbuf[...], NT_DIMS, preferred_element_type=F32) + bias_ref[...]
    n_heads = rows // t_new
    logits = jnp.where(sel[None], logits.reshape(n_heads, t_new, lp), -jnp.inf)
    m = jnp.maximum(jnp.max(logits, axis=-1, keepdims=True), NEG)
    p = jnp.exp2(logits - m)
    l = jnp.sum(p, axis=-1, keepdims=True)
    o = jnp.dot(p.reshape(rows, lp).astype(BF16), vbuf[...], preferred_element_type=F32)
    o = o.reshape(n_heads, t_new, dkv) * (1.0 / l)
    for kv in range(N_KV):
        for g in range(gqa):
            hh = kv * gqa + g
            o_ref[0, :, hh * HEAD_DIM:(hh + 1) * HEAD_DIM] = o[hh, :, kv * HEAD_DIM:(kv + 1) * HEAD_DIM]


def sample_attention(page_table, qi, wi, q, ki_new, k_new, v_new, bias, cache_ki, cache_k, cache_v,
                     t_new, n_sel):
    b, rows, _ = q.shape
    n_heads = rows // t_new
    gqa = n_heads // N_KV
    n_pages = page_table.shape[1]
    page = cache_ki.shape[1]
    dkv = N_KV * HEAD_DIM
    lp = bias.shape[1]
    n_new = ki_new.shape[1]
    per_seq = lambda shape: pl.BlockSpec((1,) + shape, lambda s, pt: (s, 0, 0))

    def page_spec(page_rows, p):
        return pl.BlockSpec((1, page_rows, HEAD_DIM), lambda s, pt: (pt[s, p], 0, 0))

    in_specs = [per_seq((qi.shape[1], D_IDX)), per_seq((wi.shape[1], 1)), per_seq((rows, HEAD_DIM)),
                per_seq((n_new, D_IDX)), per_seq((n_new, dkv)), per_seq((n_new, dkv)),
                pl.BlockSpec((rows, lp), lambda s, pt: (0, 0))]
    in_specs += [page_spec(page, p) for p in range(n_pages)]
    in_specs += [page_spec(page * N_KV, p) for p in range(n_pages)]
    in_specs += [page_spec(page * N_KV, p) for p in range(n_pages)]
    grid_spec = pltpu.PrefetchScalarGridSpec(
        num_scalar_prefetch=1,
        grid=(b,),
        in_specs=in_specs,
        out_specs=pl.BlockSpec((1, t_new, n_heads * HEAD_DIM), lambda s, pt: (s, 0, 0)),
        scratch_shapes=[pltpu.VMEM((lp, D_IDX), BF16), pltpu.VMEM((lp, dkv), BF16),
                        pltpu.VMEM((lp, dkv), BF16)],
    )
    return pl.pallas_call(
        functools.partial(_sattn_kernel, n_pages=n_pages, page=page, n_sel=n_sel, gqa=gqa),
        grid_spec=grid_spec,
        out_shape=jax.ShapeDtypeStruct((b, t_new, n_heads * HEAD_DIM), F32),
        compiler_params=_params("parallel"),
        name="sample_attention",
    )(page_table, qi, wi, q, ki_new, k_new, v_new, bias,
      *([cache_ki] * n_pages), *([cache_k] * n_pages), *([cache_v] * n_pages))


def _pad_cols(w, n):
    return jnp.pad(w, ((0, 0), (0, n - w.shape[1])))


def _prompt_bias(rel_bias, gqa):
    n_buckets = rel_bias.shape[0]
    n_kinds = len(NEAR_TILE_OFFSETS)
    kl = np.arange(TK)[:, None]
    ql = np.arange(TQ)[None, :]
    dist = np.stack([np.maximum(ql - (off + kl), 0) for off in NEAR_TILE_OFFSETS])
    bucket = _t5_bucket_np(dist, n_buckets).reshape(n_kinds * TK, TQ)
    far = int(_t5_bucket_np(np.array([MAX_DISTANCE + 1]), n_buckets)[0])
    tiles = bias_tiles(rel_bias, bucket, far, LOG2E)
    tiles = tiles.reshape(N_KV, gqa, n_kinds, TK, TQ).transpose(2, 0, 3, 1, 4)
    return tiles.reshape(n_kinds, N_KV, TK, gqa * TQ)


def _sample_bias(rel_bias, past, t_new, lp):
    n_buckets, n_heads = rel_bias.shape
    dist = np.maximum(past + np.arange(t_new)[:, None] - np.arange(lp)[None, :], 0)
    tiles = bias_tiles(rel_bias, _t5_bucket_np(dist, n_buckets), None, LOG2E)
    return tiles.reshape(n_heads * t_new, lp)


def _mixer_in(x, mods, g_pre_mix, w, ln_g, ln_b, sizes):
    d_rnn, d_q, d_kv, d_qi = sizes
    h1 = prenorm(x, g_pre_mix, mods[1], mods[0])
    c_q = 2 * d_rnn
    c_kv = c_q + d_q
    c_qi = c_kv + 2 * d_kv
    proj = functools.partial(matmul_w32, h1, w["in_t"], w_is_transposed=True)
    z_rg = proj(0, c_q, F32, name="in_proj_rg")
    q = proj(c_q, d_q, BF16, scale=HEAD_DIM ** -0.5 * LOG2E, name="in_proj_q")
    kv = proj(c_kv, 2 * d_kv, F32, name="in_proj_kv")
    qi = proj(c_qi, d_qi, BF16, name="in_proj_qi")
    n_idx = d_qi // D_IDX
    ki, wi = indexer_key_proj(h1, w["kiwi"], ln_g, ln_b, float(d_qi) ** -0.5)
    return z_rg, q, kv[:, :d_kv], kv[:, d_kv:], qi, ki, wi[:, :n_idx]


def _block_out(x, y_rnn, y_attn, mods, w, g_out_rnn, g_out_attn, g_post_mix, g_pre_ffn, g_post_ffn):
    ycat = group_norm_concat(y_rnn, y_attn, g_out_rnn, g_out_attn)
    mixed = matmul_w32(ycat, w["out"], 0, w["out"].shape[1], F32, name="out_proj")
    x1, h2 = mid_block(x, mixed, g_post_mix, mods[2], g_pre_ffn, mods[4], mods[3])
    hf = swiglu(h2, w["ffn_gate"], w["ffn_up"])
    f = matmul_ksplit(hf, w["ffn_down"], k_steps=2, name="ffn_down")
    return final_block(x1, f, g_post_ffn, mods[5])


def kernel(x_prompt, x_sample, cache_k, cache_v, cache_idx_k, state_conv, state_h, page_table,
           c_prompt, c_sample, rel_bias, w_mod, b_mod, g_pre_mix, g_post_mix, g_pre_ffn, g_post_ffn,
           w_in, conv_w, conv_b, w_rg_a, b_rg_a, w_rg_i, b_rg_i, lru_lambda, idx_k_ln_g, idx_k_ln_b,
           g_out_rnn, g_out_attn, w_out, w_ffn_gate, w_ffn_up, w_ffn_down):
    depth = w_in.shape[0]
    assert depth == 1, "one layer: prompt/sample caches are not threaded through deeper stacks"
    bp, t_p, d = x_prompt.shape
    assert bp == 1, "the prompt group is one sequence"
    bs, t_s, _ = x_sample.shape
    d_rnn = w_rg_a.shape[1] * w_rg_a.shape[2]
    d_attn = d - d_rnn
    n_heads = d_attn // HEAD_DIM
    gqa = n_heads // N_KV
    d_kv = N_KV * HEAD_DIM
    d_qi = N_IDX_HEADS * D_IDX
    n_pages = page_table.shape[1]
    page = cache_k.shape[2]
    past = n_pages * page
    width = conv_w.shape[1]
    sizes = (d_rnn, d_attn, d_kv, d_qi)

    w_in0 = w_in[0]
    c3 = 2 * d_rnn + d_attn + 2 * d_kv + d_qi
    w = {
        "in_t": jnp.swapaxes(w_in0, 0, 1),
        "kiwi": _pad_cols(w_in0[:, c3:], D_IDX + LANES).astype(BF16),
        "out": w_out[0],
        "ffn_gate": w_ffn_gate[0],
        "ffn_up": w_ffn_up[0],
        "ffn_down": w_ffn_down[0].astype(BF16),
    }

    n_c = _round_up(bs + 1, BF16_ROWS)
    c_all = jnp.concatenate([c_sample, c_prompt, jnp.zeros((n_c - bs - 1, d), F32)], axis=0)
    mod_all = modulation(c_all, w_mod[0], b_mod[0])
    mods_p = [mod_all[bs:bs + 1, j * d:(j + 1) * d] for j in range(6)]
    mods_s = [jnp.repeat(mod_all[:bs, j * d:(j + 1) * d], t_s, axis=0) for j in range(6)]

    rnn_w = (conv_w[0], conv_b[0], w_rg_a[0], b_rg_a[0], w_rg_i[0], b_rg_i[0], lru_lambda[0])
    out_w = (g_out_rnn[0], g_out_attn[0], g_post_mix[0], g_pre_ffn[0], g_post_ffn[0])

    xp = x_prompt.reshape(t_p, d)
    z_rg, q, k, v, qi, ki, wi = _mixer_in(xp, mods_p, g_pre_mix[0], w, idx_k_ln_g[0], idx_k_ln_b[0], sizes)
    y_rnn, h_p = rglru_prompt(z_rg, *rnn_w)
    n_sel_p = min(TOPK_MAX, t_p // 4)
    v_heads = v.astype(BF16).reshape(t_p, N_KV, HEAD_DIM)
    v_ones = jnp.ones((t_p, N_KV, VT_ROWS - HEAD_DIM), BF16)
    v_t = jnp.concatenate([v_heads, v_ones], axis=2).reshape(t_p, N_KV * VT_ROWS).T
    y_attn = prompt_attention(q, qi, wi.T, ki.astype(BF16), k.astype(BF16), v_t,
                              _prompt_bias(rel_bias, gqa), n_sel_p)
    y_p = _block_out(xp, y_rnn, y_attn, mods_p, w, *out_w)
    k_prompt = k.reshape(1, 1, t_p, N_KV, HEAD_DIM)
    v_prompt = v.reshape(1, 1, t_p, N_KV, HEAD_DIM)
    kidx_prompt = ki.reshape(1, 1, t_p, D_IDX)
    conv_prompt = z_rg[t_p - (width - 1):, :d_rnn].reshape(1, 1, width - 1, d_rnn)
    h_prompt = h_p.reshape(1, 1, d_rnn)

    m_s = bs * t_s
    xs = x_sample.reshape(m_s, d)
    z_rg, q, k, v, qi, ki, wi = _mixer_in(xs, mods_s, g_pre_mix[0], w, idx_k_ln_g[0], idx_k_ln_b[0], sizes)
    z_t = z_rg.reshape(bs, t_s, 2 * d_rnn).transpose(1, 0, 2)
    y_t, h_s = rglru_sample(z_t, state_conv[0].transpose(1, 0, 2), state_h[0], *rnn_w)
    y_rnn = y_t.transpose(1, 0, 2).reshape(m_s, d_rnn)

    def head_major(a, n):
        return a.reshape(bs, t_s, n, a.shape[1] // n).transpose(0, 2, 1, 3).reshape(bs, n * t_s, -1)

    n_new = _round_up(t_s, BF16_ROWS)

    def new_rows(a):
        a = a.reshape(bs, t_s, -1).astype(BF16)
        return jnp.pad(a, ((0, 0), (0, n_new - t_s), (0, 0)))

    lp = past + _round_up(t_s, LANES)
    n_sel_s = min(TOPK_MAX, (past + t_s) // 4)
    wi_rows = wi.reshape(bs, t_s, N_IDX_HEADS).transpose(0, 2, 1).reshape(bs, N_IDX_HEADS * t_s, 1)
    y_attn = sample_attention(
        page_table, head_major(qi, N_IDX_HEADS), wi_rows, head_major(q, n_heads),
        new_rows(ki), new_rows(k), new_rows(v), _sample_bias(rel_bias, past, t_s, lp),
        cache_idx_k[0], cache_k[0].reshape(-1, page * N_KV, HEAD_DIM),
        cache_v[0].reshape(-1, page * N_KV, HEAD_DIM),
        t_s, n_sel_s).reshape(m_s, d_attn)
    y_s = _block_out(xs, y_rnn, y_attn, mods_s, w, *out_w)
    k_sample = k.reshape(1, bs, t_s, N_KV, HEAD_DIM)
    v_sample = v.reshape(1, bs, t_s, N_KV, HEAD_DIM)
    kidx_sample = ki.reshape(1, bs, t_s, D_IDX)
    xr_s = z_rg[:, :d_rnn].reshape(bs, t_s, d_rnn)
    hist = jnp.concatenate([state_conv[0], xr_s], axis=1)[:, t_s:]
    conv_sample = hist.reshape(1, bs, width - 1, d_rnn)
    h_sample = h_s.reshape(1, bs, d_rnn)

    return (y_p.reshape(1, t_p, d), y_s.reshape(bs, t_s, d), k_prompt, v_prompt, kidx_prompt,
            conv_prompt, h_prompt, k_sample, v_sample, kidx_sample, conv_sample, h_sample)
```

```python
import functools
import math

import jax
import jax.numpy as jnp
import numpy as np
from jax import lax
from jax.experimental import pallas as pl
from jax.experimental.pallas import tpu as pltpu

HEAD_DIM = 128
N_KV = 4
N_IDX_HEADS = 16
D_IDX = 128
TOPK_MAX = 256
MAX_DISTANCE = 128
LRU_C = 8.0
EPS = 1e-6
NEG = -1e30

LANES = 128
SUBLANES = 8
BF16_ROWS = 16
VMEM_LIMIT_BYTES = 56 * 1024 * 1024
ROWWISE_TILE = 128

F32 = jnp.float32
BF16 = jnp.bfloat16
I32 = jnp.int32
INT_MIN = -(2 ** 31)
NT_DIMS = (((1,), (1,)), ((), ()))


def _sortable_key_of(x):
    b = int(np.array(x, np.float32).view(np.int32))
    return b ^ ((b >> 31) & 0x7FFFFFFF)


NEG_KEY = _sortable_key_of(NEG)


def _params(*semantics):
    return pltpu.CompilerParams(dimension_semantics=semantics, vmem_limit_bytes=VMEM_LIMIT_BYTES)


def _round_up(x, m):
    return -(-x // m) * m


def _row_tile(m, cap):
    t = min(m, cap)
    while m % t:
        t //= 2
    return t


def _sortable_keys(x):
    bits = pltpu.bitcast(x, I32)
    return bits ^ ((bits >> 31) & 0x7FFFFFFF)


def _rms(x, g):
    return x * lax.rsqrt(jnp.mean(x * x, axis=-1, keepdims=True) + EPS) * g


def _mod_kernel(c_ref, w_ref, b_ref, o_ref, a_s):
    @pl.when(pl.program_id(0) == 0)
    def _():
        c = c_ref[...]
        a_s[...] = (c * jax.nn.sigmoid(c)).astype(BF16)

    acc = jnp.dot(a_s[...], w_ref[...].astype(BF16), preferred_element_type=F32)
    o_ref[...] = acc + b_ref[...]


def modulation(c, w_mod, b_mod):
    b, d = c.shape
    n = w_mod.shape[1]
    tn = 512
    return pl.pallas_call(
        _mod_kernel,
        grid=(n // tn,),
        in_specs=[pl.BlockSpec((b, d), lambda j: (0, 0)),
                  pl.BlockSpec((d, tn), lambda j: (0, j)),
                  pl.BlockSpec((1, tn), lambda j: (0, j))],
        out_specs=pl.BlockSpec((b, tn), lambda j: (0, j)),
        out_shape=jax.ShapeDtypeStruct((b, n), F32),
        scratch_shapes=[pltpu.VMEM((b, d), BF16)],
        compiler_params=_params("arbitrary"),
        name="modulation",
    )(c, w_mod, b_mod.reshape(1, n))


def _mod_spec(mod, tm, d):
    if mod.shape[0] == 1:
        return pl.BlockSpec((1, d), lambda i: (0, 0))
    return pl.BlockSpec((tm, d), lambda i: (i, 0))


def _prenorm_kernel(x_ref, g_ref, sc_ref, sh_ref, o_ref):
    y = _rms(x_ref[...], g_ref[...])
    o_ref[...] = (y * (1.0 + sc_ref[...]) + sh_ref[...]).astype(o_ref.dtype)


def prenorm(x, g, scale, shift):
    m, d = x.shape
    tm = _row_tile(m, ROWWISE_TILE * (4 if scale.shape[0] == 1 else 1))
    return pl.pallas_call(
        _prenorm_kernel,
        grid=(m // tm,),
        in_specs=[pl.BlockSpec((tm, d), lambda i: (i, 0)),
                  pl.BlockSpec((1, d), lambda i: (0, 0)),
                  _mod_spec(scale, tm, d), _mod_spec(shift, tm, d)],
        out_specs=pl.BlockSpec((tm, d), lambda i: (i, 0)),
        out_shape=jax.ShapeDtypeStruct((m, d), BF16),
        compiler_params=_params("parallel"),
        name="prenorm",
    )(x, g.reshape(1, d), scale, shift)


def _group_norm_kernel(a_ref, b_ref, ga_ref, gb_ref, o_ref):
    da = a_ref.shape[1]
    o_ref[:, :da] = _rms(a_ref[...], ga_ref[...]).astype(o_ref.dtype)
    o_ref[:, da:] = _rms(b_ref[...], gb_ref[...]).astype(o_ref.dtype)


def group_norm_concat(a, b, ga, gb):
    m, da = a.shape
    db = b.shape[1]
    tm = _row_tile(m, ROWWISE_TILE * 4)
    return pl.pallas_call(
        _group_norm_kernel,
        grid=(m // tm,),
        in_specs=[pl.BlockSpec((tm, da), lambda i: (i, 0)),
                  pl.BlockSpec((tm, db), lambda i: (i, 0)),
                  pl.BlockSpec((1, da), lambda i: (0, 0)),
                  pl.BlockSpec((1, db), lambda i: (0, 0))],
        out_specs=pl.BlockSpec((tm, da + db), lambda i: (i, 0)),
        out_shape=jax.ShapeDtypeStruct((m, da + db), BF16),
        compiler_params=_params("parallel"),
        name="group_norm_concat",
    )(a, b, ga.reshape(1, da), gb.reshape(1, db))


def _mid_kernel(x_ref, mix_ref, gpost_ref, gate_ref, gpre_ref, sc_ref, sh_ref, x1_ref, h_ref):
    x1 = x_ref[...] + gate_ref[...] * _rms(mix_ref[...], gpost_ref[...])
    x1_ref[...] = x1
    h_ref[...] = (_rms(x1, gpre_ref[...]) * (1.0 + sc_ref[...]) + sh_ref[...]).astype(h_ref.dtype)


def mid_block(x, mixed, g_post, gate1, g_pre, scale2, shift2):
    m, d = x.shape
    tm = _row_tile(m, ROWWISE_TILE)
    row = pl.BlockSpec((tm, d), lambda i: (i, 0))
    vec = pl.BlockSpec((1, d), lambda i: (0, 0))
    return pl.pallas_call(
        _mid_kernel,
        grid=(m // tm,),
        in_specs=[row, row, vec, _mod_spec(gate1, tm, d), vec,
                  _mod_spec(scale2, tm, d), _mod_spec(shift2, tm, d)],
        out_specs=[row, row],
        out_shape=[jax.ShapeDtypeStruct((m, d), F32), jax.ShapeDtypeStruct((m, d), BF16)],
        compiler_params=_params("parallel"),
        name="mid_block",
    )(x, mixed, g_post.reshape(1, d), gate1, g_pre.reshape(1, d), scale2, shift2)


def _final_kernel(x_ref, f_ref, g_ref, gate_ref, o_ref):
    o_ref[...] = x_ref[...] + gate_ref[...] * _rms(f_ref[...], g_ref[...])


def final_block(x1, f, g_post, gate2):
    m, d = x1.shape
    tm = _row_tile(m, ROWWISE_TILE * 2)
    row = pl.BlockSpec((tm, d), lambda i: (i, 0))
    return pl.pallas_call(
        _final_kernel,
        grid=(m // tm,),
        in_specs=[row, row, pl.BlockSpec((1, d), lambda i: (0, 0)), _mod_spec(gate2, tm, d)],
        out_specs=row,
        out_shape=jax.ShapeDtypeStruct((m, d), F32),
        compiler_params=_params("parallel"),
        name="final_block",
    )(x1, f, g_post.reshape(1, d), gate2)


def _mm_w32_kernel(a_ref, w_ref, o_ref, wbf_s, *, scale, w_is_transposed):
    @pl.when(pl.program_id(1) == 0)
    def _():
        wbf_s[...] = w_ref[...].astype(BF16)

    if w_is_transposed:
        acc = lax.dot_general(a_ref[...], wbf_s[...], NT_DIMS, preferred_element_type=F32)
    else:
        acc = jnp.dot(a_ref[...], wbf_s[...], preferred_element_type=F32)
    if scale != 1.0:
        acc = acc * scale
    o_ref[...] = acc.astype(o_ref.dtype)


def matmul_w32(a, w, col0, n, out_dtype, *, scale=1.0, w_is_transposed=False, name="matmul_w32"):
    m, k = a.shape
    tm = _row_tile(m, 1024)
    tn = _row_tile(n, 512)
    assert col0 % tn == 0, (col0, tn)
    j0 = col0 // tn
    if w_is_transposed:
        w_block = (tn, k)
        w_spec = pl.BlockSpec(w_block, lambda j, i: (j0 + j, 0))
    else:
        w_block = (k, tn)
        w_spec = pl.BlockSpec(w_block, lambda j, i: (0, j0 + j))
    return pl.pallas_call(
        functools.partial(_mm_w32_kernel, scale=scale, w_is_transposed=w_is_transposed),
        grid=(n // tn, m // tm),
        in_specs=[pl.BlockSpec((tm, k), lambda j, i: (i, 0)), w_spec],
        out_specs=pl.BlockSpec((tm, tn), lambda j, i: (i, j)),
        out_shape=jax.ShapeDtypeStruct((m, n), out_dtype),
        scratch_shapes=[pltpu.VMEM(w_block, BF16)],
        compiler_params=_params("parallel", "arbitrary"),
        name=name,
    )(a, w)


def _mm_acc_kernel(a_ref, w_ref, o_ref, acc_s):
    kk = pl.program_id(2)

    @pl.when(kk == 0)
    def _():
        acc_s[...] = jnp.zeros_like(acc_s)

    acc_s[...] += jnp.dot(a_ref[...], w_ref[...], preferred_element_type=F32)

    @pl.when(kk == pl.num_programs(2) - 1)
    def _():
        o_ref[...] = acc_s[...]


def matmul_ksplit(a, w, *, k_steps, name="matmul_ksplit"):
    m, k = a.shape
    n = w.shape[1]
    tm = _row_tile(m, 1024)
    tn = _row_tile(n, 512)
    tk = k // k_steps
    return pl.pallas_call(
        _mm_acc_kernel,
        grid=(m // tm, n // tn, k_steps),
        in_specs=[pl.BlockSpec((tm, tk), lambda i, j, s: (i, s)),
                  pl.BlockSpec((tk, tn), lambda i, j, s: (s, j))],
        out_specs=pl.BlockSpec((tm, tn), lambda i, j, s: (i, j)),
        out_shape=jax.ShapeDtypeStruct((m, n), F32),
        scratch_shapes=[pltpu.VMEM((tm, tn), F32)],
        compiler_params=_params("parallel", "arbitrary", "arbitrary"),
        name=name,
    )(a, w)


def _swiglu_kernel(a_ref, wg_ref, wu_ref, o_ref, wg_s, wu_s):
    @pl.when(pl.program_id(1) == 0)
    def _():
        wg_s[...] = wg_ref[...].astype(BF16)
        wu_s[...] = wu_ref[...].astype(BF16)

    a = a_ref[...]
    g = jnp.dot(a, wg_s[...], preferred_element_type=F32)
    u = jnp.dot(a, wu_s[...], preferred_element_type=F32)
    o_ref[...] = (g * jax.nn.sigmoid(g) * u).astype(o_ref.dtype)


def swiglu(a, wg, wu):
    m, k = a.shape
    n = wg.shape[1]
    tm = _row_tile(m, 1024)
    tn = _row_tile(n, 256)
    wspec = pl.BlockSpec((k, tn), lambda j, i: (0, j))
    return pl.pallas_call(
        _swiglu_kernel,
        grid=(n // tn, m // tm),
        in_specs=[pl.BlockSpec((tm, k), lambda j, i: (i, 0)), wspec, wspec],
        out_specs=pl.BlockSpec((tm, tn), lambda j, i: (i, j)),
        out_shape=jax.ShapeDtypeStruct((m, n), BF16),
        scratch_shapes=[pltpu.VMEM((k, tn), BF16), pltpu.VMEM((k, tn), BF16)],
        compiler_params=_params("parallel", "arbitrary"),
        name="swiglu",
    )(a, wg, wu)


def _idxkey_kernel(a_ref, w_ref, g_ref, b_ref, ki_ref, wi_ref, *, wi_scale):
    z = jnp.dot(a_ref[...], w_ref[...], preferred_element_type=F32)
    ki = z[:, :D_IDX]
    mu = jnp.mean(ki, axis=-1, keepdims=True)
    var = jnp.mean(jnp.square(ki - mu), axis=-1, keepdims=True)
    ki_ref[...] = (ki - mu) * lax.rsqrt(var + EPS) * g_ref[...] + b_ref[...]
    wi_ref[...] = z[:, D_IDX:] * wi_scale


def indexer_key_proj(a, w, ln_g, ln_b, wi_scale):
    m, k = a.shape
    n = w.shape[1]
    tm = _row_tile(m, 512)
    return pl.pallas_call(
        functools.partial(_idxkey_kernel, wi_scale=wi_scale),
        grid=(m // tm,),
        in_specs=[pl.BlockSpec((tm, k), lambda i: (i, 0)),
                  pl.BlockSpec((k, n), lambda i: (0, 0)),
                  pl.BlockSpec((1, D_IDX), lambda i: (0, 0)),
                  pl.BlockSpec((1, D_IDX), lambda i: (0, 0))],
        out_specs=[pl.BlockSpec((tm, D_IDX), lambda i: (i, 0)),
                   pl.BlockSpec((tm, n - D_IDX), lambda i: (i, 0))],
        out_shape=[jax.ShapeDtypeStruct((m, D_IDX), F32),
                   jax.ShapeDtypeStruct((m, n - D_IDX), F32)],
        compiler_params=_params("parallel"),
        name="indexer_key_proj",
    )(a, w, ln_g.reshape(1, D_IDX), ln_b.reshape(1, D_IDX))


def _rg_gates(conv, wa_ref, ba_ref, wi_ref, bi_ref, lam_ref):
    cb = conv.astype(BF16)
    r = jax.nn.sigmoid(jnp.dot(cb, wa_ref[0], preferred_element_type=F32) + ba_ref[...])
    i = jax.nn.sigmoid(jnp.dot(cb, wi_ref[0], preferred_element_type=F32) + bi_ref[...])
    nl = -lam_ref[...]
    softplus = jnp.maximum(nl, 0.0) + jnp.log1p(jnp.exp(-jnp.abs(nl)))
    log_a = -LRU_C * r * softplus
    a = jnp.exp(log_a)
    u = jnp.sqrt(-jnp.tanh(log_a) * (a * a + 1.0)) * (i * conv)
    return a, u


def _rglru_prompt_kernel(xr_ref, gt_ref, cw_ref, cb_ref, wa_ref, ba_ref, wi_ref, bi_ref, lam_ref,
                         y_ref, hlast_ref, xbuf, hc):
    tt = xr_ref.shape[0]
    hist = SUBLANES

    @pl.when(pl.program_id(1) == 0)
    def _():
        xbuf[0:hist, :] = jnp.zeros((hist, LANES), F32)
        hc[...] = jnp.zeros_like(hc)

    xbuf[hist:hist + tt, :] = xr_ref[...]
    width = cw_ref.shape[0]
    conv = cb_ref[...] + xbuf[hist - width + 1:hist - width + 1 + tt, :] * cw_ref[0:1, :]
    for j in range(1, width):
        s = hist - width + 1 + j
        conv = conv + xbuf[s:s + tt, :] * cw_ref[j:j + 1, :]
    xbuf[0:hist, :] = xbuf[tt:tt + hist, :]

    a, u = _rg_gates(conv, wa_ref, ba_ref, wi_ref, bi_ref, lam_ref)

    row = lax.broadcasted_iota(I32, (tt, LANES), 0)
    d = 1
    while d < tt:
        keep = row >= d
        a_sh = jnp.where(keep, pltpu.roll(a, d, 0), 1.0)
        u_sh = jnp.where(keep, pltpu.roll(u, d, 0), 0.0)
        u = a * u_sh + u
        a = a * a_sh
        d *= 2
    h = u + a * hc[...]
    hc[...] = h[tt - 1:tt, :]
    hlast_ref[...] = h[tt - 1:tt, :]
    y_ref[...] = h * jax.nn.gelu(gt_ref[...])


def rglru_prompt(z_rg, conv_w, conv_b, w_a, b_a, w_i, b_i, lam):
    t, c2 = z_rg.shape
    c = c2 // 2
    nb = c // LANES
    tt = _row_tile(t, 1024)
    width = conv_w.shape[0]
    vec = pl.BlockSpec((1, LANES), lambda n, i: (0, n))
    wsp = pl.BlockSpec((1, LANES, LANES), lambda n, i: (n, 0, 0))
    return pl.pallas_call(
        _rglru_prompt_kernel,
        grid=(nb, t // tt),
        in_specs=[pl.BlockSpec((tt, LANES), lambda n, i: (i, n)),
                  pl.BlockSpec((tt, LANES), lambda n, i: (i, nb + n)),
                  pl.BlockSpec((width, LANES), lambda n, i: (0, n)),
                  vec, wsp, vec, wsp, vec, vec],
        out_specs=[pl.BlockSpec((tt, LANES), lambda n, i: (i, n)), vec],
        out_shape=[jax.ShapeDtypeStruct((t, c), F32), jax.ShapeDtypeStruct((1, c), F32)],
        scratch_shapes=[pltpu.VMEM((tt + SUBLANES, LANES), F32), pltpu.VMEM((1, LANES), F32)],
        compiler_params=_params("parallel", "arbitrary"),
        name="rglru_prompt",
    )(z_rg, z_rg, conv_w, conv_b.reshape(1, c), w_a.astype(BF16), b_a.reshape(1, c),
      w_i.astype(BF16), b_i.reshape(1, c), lam.reshape(1, c))


def _rglru_sample_kernel(xr_ref, gt_ref, buf_ref, h0_ref, cw_ref, cb_ref, wa_ref, ba_ref, wi_ref,
                         bi_ref, lam_ref, y_ref, hlast_ref, hist, hc):
    width = cw_ref.shape[0]

    @pl.when(pl.program_id(1) == 0)
    def _():
        hist[...] = buf_ref[...]
        hc[...] = h0_ref[...]

    x = xr_ref[0]
    conv = cb_ref[...] + hist[0] * cw_ref[0:1, :]
    for j in range(1, width - 1):
        conv = conv + hist[j] * cw_ref[j:j + 1, :]
    conv = conv + x * cw_ref[width - 1:width, :]
    for j in range(width - 2):
        hist[j] = hist[j + 1]
    hist[width - 2] = x

    a, u = _rg_gates(conv, wa_ref, ba_ref, wi_ref, bi_ref, lam_ref)
    h = a * hc[...] + u
    hc[...] = h
    hlast_ref[...] = h
    y_ref[0] = h * jax.nn.gelu(gt_ref[0])


def rglru_sample(z_rg, buf, h0, conv_w, conv_b, w_a, b_a, w_i, b_i, lam):
    t, b, c2 = z_rg.shape
    c = c2 // 2
    nb = c // LANES
    width = conv_w.shape[0]
    vec = pl.BlockSpec((1, LANES), lambda n, i: (0, n))
    wsp = pl.BlockSpec((1, LANES, LANES), lambda n, i: (n, 0, 0))
    st = pl.BlockSpec((b, LANES), lambda n, i: (0, n))
    return pl.pallas_call(
        _rglru_sample_kernel,
        grid=(nb, t),
        in_specs=[pl.BlockSpec((1, b, LANES), lambda n, i: (i, 0, n)),
                  pl.BlockSpec((1, b, LANES), lambda n, i: (i, 0, nb + n)),
                  pl.BlockSpec((width - 1, b, LANES), lambda n, i: (0, 0, n)),
                  st,
                  pl.BlockSpec((width, LANES), lambda n, i: (0, n)),
                  vec, wsp, vec, wsp, vec, vec],
        out_specs=[pl.BlockSpec((1, b, LANES), lambda n, i: (i, 0, n)), st],
        out_shape=[jax.ShapeDtypeStruct((t, b, c), F32), jax.ShapeDtypeStruct((b, c), F32)],
        scratch_shapes=[pltpu.VMEM((width - 1, b, LANES), F32), pltpu.VMEM((b, LANES), F32)],
        compiler_params=_params("parallel", "arbitrary"),
        name="rglru_sample",
    )(z_rg, z_rg, buf, h0, conv_w, conv_b.reshape(1, c), w_a.astype(BF16), b_a.reshape(1, c),
      w_i.astype(BF16), b_i.reshape(1, c), lam.reshape(1, c))


def _t5_bucket_np(dist, n_buckets):
    max_exact = n_buckets // 2
    d = np.maximum(dist, 1).astype(np.float32)
    large = max_exact + (np.log(d / np.float32(max_exact)) / np.float32(math.log(MAX_DISTANCE / max_exact))
                         * np.float32(n_buckets - max_exact)).astype(np.int32)
    large = np.minimum(large, n_buckets - 1)
    return np.where(dist < max_exact, dist, large).astype(np.int32)


def _bias_kernel(rb_ref, bucket_ref, o_ref, *, n_buckets, far_bucket, scale):
    h = pl.program_id(0)
    bucket = bucket_ref[...]
    shift = rb_ref[far_bucket, h] if far_bucket is not None else 0.0
    acc = jnp.zeros(bucket.shape, F32)
    for b in range(n_buckets):
        acc = jnp.where(bucket == b, (rb_ref[b, h] - shift) * scale, acc)
    o_ref[0] = acc


def bias_tiles(rel_bias, bucket, far_bucket, scale):
    n_buckets, n_heads = rel_bias.shape
    r, c = bucket.shape
    return pl.pallas_call(
        functools.partial(_bias_kernel, n_buckets=n_buckets, far_bucket=far_bucket, scale=scale),
        grid=(n_heads,),
        in_specs=[pl.BlockSpec(memory_space=pltpu.SMEM),
                  pl.BlockSpec((r, c), lambda h: (0, 0))],
        out_specs=pl.BlockSpec((1, r, c), lambda h: (h, 0, 0)),
        out_shape=jax.ShapeDtypeStruct((n_heads, r, c), F32),
        compiler_params=_params("parallel"),
        name="bias_tiles",
    )(rel_bias, jnp.asarray(bucket))


TQ = 128
TK = 256
TK_FULL = 512
LOG2E = math.log2(math.e)
VT_ROWS = HEAD_DIM + BF16_ROWS
NEAR_TILE_OFFSETS = (-TQ, 0, -TK)


def _pattn_kernel(qi_ref, wit_ref, q_ref, kidx_ref, k_ref, vt_ref, bias_ref, o_ref,
                  keys_s, qis_s, qs_s, s0_s, s1_s, acc_s, m_s, last_tie_s, *, seq_len, n_sel, gqa):
    i = pl.program_id(0)
    pos_bits = seq_len.bit_length()
    n_idx = qis_s.shape[0] // TQ

    for h in range(n_idx):
        qis_s[h * TQ:(h + 1) * TQ, :] = qi_ref[:, h * D_IDX:(h + 1) * D_IDX]
    for kv in range(N_KV):
        for g in range(gqa):
            hh = kv * gqa + g
            qs_s[kv, g * TQ:(g + 1) * TQ, :] = q_ref[:, hh * HEAD_DIM:(hh + 1) * HEAD_DIM]

    qpos = i * TQ + lax.broadcasted_iota(I32, (1, TQ), 1)

    fulls_per_tile = TK_FULL // TQ
    n_full = i // fulls_per_tile
    n_part = (i % fulls_per_tile) // (TK // TQ) + 1
    part0 = n_full * TK_FULL

    def score_tile(k0, tk, masked):
        kt = kidx_ref[pl.ds(k0, tk), :]
        acc = jnp.zeros((tk, TQ), F32)
        for hp in range(n_idx // 2):
            s = lax.dot_general(kt, qis_s[hp * 2 * TQ:(hp + 1) * 2 * TQ, :], NT_DIMS,
                                preferred_element_type=F32)
            for e in range(2):
                h = 2 * hp + e
                acc = acc + jnp.maximum(s[:, e * TQ:(e + 1) * TQ], 0.0) * wit_ref[h:h + 1, :]
        if masked:
            kpos = k0 + lax.broadcasted_iota(I32, (tk, 1), 0)
            acc = jnp.where(kpos <= qpos, acc, NEG)
        keys_s[pl.ds(k0, tk), :] = _sortable_keys(acc)

    def full_scores(j, c):
        score_tile(pl.multiple_of(j * TK_FULL, TK_FULL), TK_FULL, False)
        return c

    def part_scores(j, c):
        score_tile(pl.multiple_of(part0 + j * TK, TK), TK, True)
        return c

    lax.fori_loop(0, n_full, full_scores, 0)
    lax.fori_loop(0, n_part, part_scores, 0)

    n_after = seq_len - (part0 + n_part * TK)

    def count_ge(k0, rows, cand, cnt):
        slab = keys_s[pl.ds(k0, rows), :]
        for c in range(rows // TQ):
            u = slab[c * TQ:(c + 1) * TQ, :]
            ge = u.reshape(TQ // SUBLANES, SUBLANES, TQ) >= cand[None]
            cnt = cnt + jnp.sum(jnp.where(ge, 1, 0), axis=0)
        return cnt

    def over_tiles(tile_fn, carry):
        carry = lax.fori_loop(
            0, n_full, lambda j, c: tile_fn(pl.multiple_of(j * TK_FULL, TK_FULL), TK_FULL, c), carry)
        return lax.fori_loop(
            0, n_part, lambda j, c: tile_fn(pl.multiple_of(part0 + j * TK, TK), TK, c), carry)

    zero_cnt = jnp.zeros((SUBLANES, TQ), I32)

    def bit_step(b, carry):
        thr, n_ge = carry
        cand = thr ^ lax.shift_left(jnp.int32(1), 31 - b)
        cnt = over_tiles(lambda k0, rows, cnt: count_ge(k0, rows, cand, cnt), zero_cnt)
        tot = jnp.sum(cnt, axis=0, keepdims=True)
        tot = tot + jnp.where(cand[0:1] <= NEG_KEY, n_after, 0)
        take = tot >= n_sel
        return jnp.where(take, cand, thr), jnp.where(take, tot, n_ge)

    thr, n_ge = lax.fori_loop(
        0, 32, bit_step,
        (jnp.full((SUBLANES, TQ), INT_MIN, I32), jnp.full((SUBLANES, TQ), seq_len, I32)))
    thr = jnp.maximum(thr[0:1], INT_MIN + 1)

    tied = (n_ge[0:1] > n_sel) & (thr > NEG_KEY)
    last_tie_s[...] = jnp.full((1, TQ), seq_len, I32)

    @pl.when(jnp.max(jnp.where(tied, 1, 0)) > 0)
    def _():
        def count_where(pred, k0, rows, cnt):
            slab = keys_s[pl.ds(k0, rows), :]
            for c in range(rows // TQ):
                u = slab[c * TQ:(c + 1) * TQ, :]
                kpos = k0 + c * TQ + lax.broadcasted_iota(I32, (TQ, 1), 0)
                hit = pred(u, kpos)
                cnt = cnt + jnp.sum(hit.reshape(TQ // SUBLANES, SUBLANES, TQ), axis=0)
            return cnt

        def total(pred):
            cnt = over_tiles(lambda k0, rows, cnt: count_where(pred, k0, rows, cnt), zero_cnt)
            return jnp.sum(cnt, axis=0, keepdims=True)

        n_tie = n_sel - total(lambda u, kpos: jnp.where(u > thr, 1, 0))

        def pos_step(b, c):
            cand = c | lax.shift_left(jnp.int32(1), pos_bits - 1 - b)
            before = total(lambda u, kpos: jnp.where(u == thr, jnp.where(kpos < cand, 1, 0), 0))
            return jnp.where(before < n_tie, cand, c)

        last_tie = lax.fori_loop(0, pos_bits, pos_step, jnp.zeros((1, TQ), I32))
        last_tie_s[...] = jnp.where(tied, last_tie, seq_len)

    last_tie = last_tie_s[...]

    m_s[...] = jnp.full(m_s.shape, NEG, F32)
    acc_s[...] = jnp.zeros(acc_s.shape, F32)

    def tile_start(j):
        return j * TK if isinstance(j, int) else pl.multiple_of(j * TK, TK)

    def tile_mask(k0, near):
        u = keys_s[pl.ds(k0, TK), :]
        kpos = k0 + lax.broadcasted_iota(I32, (TK, 1), 0)
        if near:
            u = jnp.where(kpos <= qpos, u, INT_MIN)
        mask = jnp.where(u >= jnp.where(kpos <= last_tie, thr, thr + 1), 0.0, -jnp.inf)
        return jnp.concatenate([mask] * gqa, axis=1)

    def logits_kv(k0, s_ref, kv, mask, near_kind):
        kt = k_ref[pl.ds(k0, TK), kv * HEAD_DIM:(kv + 1) * HEAD_DIM]
        s = lax.dot_general(kt, qs_s[kv], NT_DIMS, preferred_element_type=F32)
        if near_kind is not None:
            s = s + bias_ref[near_kind, kv]
        s_ref[kv] = s + mask

    def softmax_kv(k0, s_ref, kv):
        m_old = m_s[kv]
        m_new = jnp.maximum(m_old, jnp.max(s_ref[kv], axis=0, keepdims=True))
        alpha = jnp.exp2(m_old - m_new)
        p = jnp.exp2(s_ref[kv] - m_new)
        vt = vt_ref[kv * VT_ROWS:(kv + 1) * VT_ROWS, pl.ds(k0, TK)]
        pv = jnp.dot(vt, p.astype(BF16), preferred_element_type=F32)
        acc_s[kv] = acc_s[kv] * alpha + pv
        m_s[kv] = m_new

    def logits_stage(j, s_ref, near_kind):
        k0 = tile_start(j)
        mask = tile_mask(k0, near_kind is not None)
        for kv in range(N_KV):
            logits_kv(k0, s_ref, kv, mask, near_kind)

    def softmax_stage(j, s_ref):
        k0 = tile_start(j)
        for kv in range(N_KV):
            softmax_kv(k0, s_ref, kv)

    def far_step(j_cur, s_cur, j_next, s_next):
        k0_cur, k0_next = tile_start(j_cur), tile_start(j_next)
        mask = tile_mask(k0_next, False)
        for kv in range(N_KV):
            logits_kv(k0_next, s_next, kv, mask, None)
            softmax_kv(k0_cur, s_cur, kv)

    n_tiles = (i + 2) // 2
    n_far = jnp.maximum(i - 1, 0) // 2
    last_far = jnp.maximum(n_far - 1, 0)
    logits_stage(0, s0_s, None)

    def far_pair(p, c):
        far_step(2 * p, s0_s, 2 * p + 1, s1_s)
        far_step(2 * p + 1, s1_s, jnp.minimum(2 * p + 2, last_far), s0_s)
        return c

    lax.fori_loop(0, n_far // 2, far_pair, 0)

    @pl.when(n_far % 2 == 1)
    def _():
        softmax_stage(n_far - 1, s0_s)

    def near_tile(j, c):
        kind = jnp.where(i % 2 == 1, 0, n_tiles - j)
        logits_stage(j, s0_s, kind)
        softmax_stage(j, s0_s)
        return c

    lax.fori_loop(n_far, n_tiles, near_tile, 0)

    for kv in range(N_KV):
        ot = acc_s[kv, :HEAD_DIM, :] * (1.0 / acc_s[kv, HEAD_DIM:HEAD_DIM + 1, :])
        for g in range(gqa):
            hh = kv * gqa + g
            o_ref[:, hh * HEAD_DIM:(hh + 1) * HEAD_DIM] = ot[:, g * TQ:(g + 1) * TQ].T


def prompt_attention(q, qi, wi_t, kidx, k, v_t, bias, n_sel):
    t, dq = q.shape
    n_heads = dq // HEAD_DIM
    gqa = n_heads // N_KV
    n_idx = qi.shape[1] // D_IDX
    dkv = N_KV * HEAD_DIM
    whole = lambda shape: pl.BlockSpec(shape, lambda i: (0,) * len(shape))
    return pl.pallas_call(
        functools.partial(_pattn_kernel, seq_len=t, n_sel=n_sel, gqa=gqa),
        grid=(t // TQ,),
        in_specs=[pl.BlockSpec((TQ, n_idx * D_IDX), lambda i: (i, 0)),
                  pl.BlockSpec((n_idx, TQ), lambda i: (0, i)),
                  pl.BlockSpec((TQ, dq), lambda i: (i, 0)),
                  whole((t, D_IDX)), whole((t, dkv)), whole((N_KV * VT_ROWS, t)),
                  whole(bias.shape)],
        out_specs=pl.BlockSpec((TQ, dq), lambda i: (i, 0)),
        out_shape=jax.ShapeDtypeStruct((t, dq), F32),
        scratch_shapes=[pltpu.VMEM((t, TQ), I32),
                        pltpu.VMEM((n_idx * TQ, D_IDX), BF16),
                        pltpu.VMEM((N_KV, gqa * TQ, HEAD_DIM), BF16),
                        pltpu.VMEM((N_KV, TK, gqa * TQ), F32),
                        pltpu.VMEM((N_KV, TK, gqa * TQ), F32),
                        pltpu.VMEM((N_KV, VT_ROWS, gqa * TQ), F32),
                        pltpu.VMEM((N_KV, 1, gqa * TQ), F32),
                        pltpu.VMEM((1, TQ), I32)],
        compiler_params=_params("parallel"),
        name="prompt_attention",
    )(qi, wi_t, q, kidx, k, v_t, bias)


RADIX_BITS = 4

def _sattn_kernel(pt_ref, qi_ref, wi_ref, q_ref, kin_ref, kn_ref, vn_ref, bias_ref, *rest,
                  n_pages, page, n_sel, gqa):
    ki_pages = rest[:n_pages]
    k_pages = rest[n_pages:2 * n_pages]
    v_pages = rest[2 * n_pages:3 * n_pages]
    o_ref, kibuf, kbuf, vbuf, logits_s, last_tie_s = rest[3 * n_pages:]
    del pt_ref
    past = n_pages * page
    lp = kibuf.shape[0]
    t_new = o_ref.shape[1]
    n_new = kin_ref.shape[1]
    rows = q_ref.shape[1]
    dkv = N_KV * HEAD_DIM

    for p in range(n_pages):
        kibuf[p * page:(p + 1) * page, :] = ki_pages[p][0].astype(BF16)
        for kv in range(N_KV):
            rows_kv = pl.ds(kv, page, stride=N_KV)
            cols_kv = slice(kv * HEAD_DIM, (kv + 1) * HEAD_DIM)
            kbuf[p * page:(p + 1) * page, cols_kv] = k_pages[p][0, rows_kv, :].astype(BF16)
            vbuf[p * page:(p + 1) * page, cols_kv] = v_pages[p][0, rows_kv, :].astype(BF16)
    kibuf[past:, :] = jnp.zeros((lp - past, D_IDX), BF16)
    kbuf[past:, :] = jnp.zeros((lp - past, dkv), BF16)
    vbuf[past:, :] = jnp.zeros((lp - past, dkv), BF16)
    kibuf[past:past + n_new, :] = kin_ref[0]
    kbuf[past:past + n_new, :] = kn_ref[0]
    vbuf[past:past + n_new, :] = vn_ref[0]

    kpos = lax.broadcasted_iota(I32, (1, lp), 1)
    qpos = past + lax.broadcasted_iota(I32, (t_new, 1), 0)
    visible = kpos <= qpos

    qrep = jnp.concatenate([q_ref[0]] * N_KV, axis=1)
    rkv = lax.broadcasted_iota(I32, (rows, dkv), 0) // (gqa * t_new)
    ckv = lax.broadcasted_iota(I32, (rows, dkv), 1) // HEAD_DIM
    qblk = jnp.where(rkv == ckv, qrep, jnp.zeros_like(qrep))
    logits_s[...] = lax.dot_general(qblk, kbuf[...], NT_DIMS, preferred_element_type=F32) + bias_ref[...]

    s = lax.dot_general(qi_ref[0], kibuf[...], NT_DIMS, preferred_element_type=F32)
    s = jnp.maximum(s, 0.0) * wi_ref[0]
    n_idx = s.shape[0] // t_new
    score = jnp.sum(s.reshape(n_idx, t_new, lp), axis=0)
    score = jnp.where(visible, score, NEG)
    keys = jnp.where(kpos < past + t_new, _sortable_keys(score), INT_MIN)

    prefix = jnp.zeros((t_new, 1), I32)
    for shift in range(32 - RADIX_BITS, -1, -RADIX_BITS):
        digit = jnp.zeros((t_new, 1), I32)
        for v in range(1, 2 ** RADIX_BITS):
            step = int(np.array((v << shift) & 0xFFFFFFFF, np.uint32).view(np.int32))
            cand = (prefix | step) ^ INT_MIN
            tot = jnp.sum(jnp.where(keys >= cand, 1, 0), axis=1, keepdims=True)
            digit = digit + jnp.where(tot >= n_sel, 1, 0)
        prefix = prefix | lax.shift_left(digit, jnp.int32(shift))
    thr = jnp.maximum(prefix ^ INT_MIN, INT_MIN + 1)

    n_ge = jnp.sum(jnp.where(keys >= thr, 1, 0), axis=1, keepdims=True)
    tied = (n_ge > n_sel) & (thr > NEG_KEY)
    last_tie_s[...] = jnp.full((t_new, 1), lp, I32)

    @pl.when(jnp.max(jnp.where(tied, 1, 0)) > 0)
    def _():
        pos_bits = lp.bit_length()
        n_tie = n_sel - jnp.sum(jnp.where(keys > thr, 1, 0), axis=1, keepdims=True)

        def pos_step(b, c):
            cand = c | lax.shift_left(jnp.int32(1), pos_bits - 1 - b)
            before = jnp.sum(jnp.where(keys == thr, jnp.where(kpos < cand, 1, 0), 0), axis=1, keepdims=True)
            return jnp.where(before < n_tie, cand, c)

        last_tie = lax.fori_loop(0, pos_bits, pos_step, jnp.zeros((t_new, 1), I32))
        last_tie_s[...] = jnp.where(tied, last_tie, lp)

    thr_at = jnp.where(kpos <= last_tie_s[...], thr, thr + 1)
    sel = jnp.where(visible, keys, INT_MIN) >= thr_at

    n_heads = rows // t_new
    logits = jnp.where(sel[None], logits_s[...].reshape(n_heads, t_new, lp), -jnp.inf)
    m = jnp.maximum(jnp.max(logits, axis=-1, keepdims=True), NEG)
    p = jnp.exp2(logits - m)
    l = jnp.sum(p, axis=-1, keepdims=True)
    o = jnp.dot(p.reshape(rows, lp).astype(BF16), vbuf[...], preferred_element_type=F32)
    o = o.reshape(n_heads, t_new, dkv) * (1.0 / l)
    for kv in range(N_KV):
        for g in range(gqa):
            hh = kv * gqa + g
            o_ref[0, :, hh * HEAD_DIM:(hh + 1) * HEAD_DIM] = o[hh, :, kv * HEAD_DIM:(kv + 1) * HEAD_DIM]


def sample_attention(page_table, qi, wi, q, ki_new, k_new, v_new, bias, cache_ki, cache_k, cache_v,
                     t_new, n_sel):
    b, rows, _ = q.shape
    n_heads = rows // t_new
    gqa = n_heads // N_KV
    n_pages = page_table.shape[1]
    page = cache_ki.shape[1]
    dkv = N_KV * HEAD_DIM
    lp = bias.shape[1]
    n_new = ki_new.shape[1]
    per_seq = lambda shape: pl.BlockSpec((1,) + shape, lambda s, pt: (s, 0, 0))

    def page_spec(page_rows, p):
        return pl.BlockSpec((1, page_rows, HEAD_DIM), lambda s, pt: (pt[s, p], 0, 0))

    in_specs = [per_seq((qi.shape[1], D_IDX)), per_seq((wi.shape[1], 1)), per_seq((rows, HEAD_DIM)),
                per_seq((n_new, D_IDX)), per_seq((n_new, dkv)), per_seq((n_new, dkv)),
                pl.BlockSpec((rows, lp), lambda s, pt: (0, 0))]
    in_specs += [page_spec(page, p) for p in range(n_pages)]
    in_specs += [page_spec(page * N_KV, p) for p in range(n_pages)]
    in_specs += [page_spec(page * N_KV, p) for p in range(n_pages)]
    grid_spec = pltpu.PrefetchScalarGridSpec(
        num_scalar_prefetch=1,
        grid=(b,),
        in_specs=in_specs,
        out_specs=pl.BlockSpec((1, t_new, n_heads * HEAD_DIM), lambda s, pt: (s, 0, 0)),
        scratch_shapes=[pltpu.VMEM((lp, D_IDX), BF16), pltpu.VMEM((lp, dkv), BF16),
                        pltpu.VMEM((lp, dkv), BF16), pltpu.VMEM((rows, lp), F32),
                        pltpu.VMEM((t_new, 1), I32)],
    )
    return pl.pallas_call(
        functools.partial(_sattn_kernel, n_pages=n_pages, page=page, n_sel=n_sel, gqa=gqa),
        grid_spec=grid_spec,
        out_shape=jax.ShapeDtypeStruct((b, t_new, n_heads * HEAD_DIM), F32),
        compiler_params=_params("parallel"),
        name="sample_attention",
    )(page_table, qi, wi, q, ki_new, k_new, v_new, bias,
      *([cache_ki] * n_pages), *([cache_k] * n_pages), *([cache_v] * n_pages))


def _pad_cols(w, n):
    return jnp.pad(w, ((0, 0), (0, n - w.shape[1])))


def _prompt_bias(rel_bias, gqa):
    n_buckets = rel_bias.shape[0]
    n_kinds = len(NEAR_TILE_OFFSETS)
    kl = np.arange(TK)[:, None]
    ql = np.arange(TQ)[None, :]
    dist = np.stack([np.maximum(ql - (off + kl), 0) for off in NEAR_TILE_OFFSETS])
    bucket = _t5_bucket_np(dist, n_buckets).reshape(n_kinds * TK, TQ)
    far = int(_t5_bucket_np(np.array([MAX_DISTANCE + 1]), n_buckets)[0])
    tiles = bias_tiles(rel_bias, bucket, far, LOG2E)
    tiles = tiles.reshape(N_KV, gqa, n_kinds, TK, TQ).transpose(2, 0, 3, 1, 4)
    return tiles.reshape(n_kinds, N_KV, TK, gqa * TQ)


def _sample_bias(rel_bias, past, t_new, lp):
    n_buckets, n_heads = rel_bias.shape
    dist = np.maximum(past + np.arange(t_new)[:, None] - np.arange(lp)[None, :], 0)
    tiles = bias_tiles(rel_bias, _t5_bucket_np(dist, n_buckets), None, LOG2E)
    return tiles.reshape(n_heads * t_new, lp)


def _mixer_in(x, mods, g_pre_mix, w, ln_g, ln_b, sizes):
    d_rnn, d_q, d_kv, d_qi = sizes
    h1 = prenorm(x, g_pre_mix, mods[1], mods[0])
    c_q = 2 * d_rnn
    c_kv = c_q + d_q
    c_qi = c_kv + 2 * d_kv
    proj = functools.partial(matmul_w32, h1, w["in_t"], w_is_transposed=True)
    z_rg = proj(0, c_q, F32, name="in_proj_rg")
    q = proj(c_q, d_q, BF16, scale=HEAD_DIM ** -0.5 * LOG2E, name="in_proj_q")
    kv = proj(c_kv, 2 * d_kv, F32, name="in_proj_kv")
    qi = proj(c_qi, d_qi, BF16, name="in_proj_qi")
    n_idx = d_qi // D_IDX
    ki, wi = indexer_key_proj(h1, w["kiwi"], ln_g, ln_b, float(d_qi) ** -0.5)
    return z_rg, q, kv[:, :d_kv], kv[:, d_kv:], qi, ki, wi[:, :n_idx]


def _block_out(x, y_rnn, y_attn, mods, w, g_out_rnn, g_out_attn, g_post_mix, g_pre_ffn, g_post_ffn):
    ycat = group_norm_concat(y_rnn, y_attn, g_out_rnn, g_out_attn)
    mixed = matmul_w32(ycat, w["out"], 0, w["out"].shape[1], F32, name="out_proj")
    x1, h2 = mid_block(x, mixed, g_post_mix, mods[2], g_pre_ffn, mods[4], mods[3])
    hf = swiglu(h2, w["ffn_gate"], w["ffn_up"])
    f = matmul_ksplit(hf, w["ffn_down"], k_steps=2, name="ffn_down")
    return final_block(x1, f, g_post_ffn, mods[5])


def kernel(x_prompt, x_sample, cache_k, cache_v, cache_idx_k, state_conv, state_h, page_table,
           c_prompt, c_sample, rel_bias, w_mod, b_mod, g_pre_mix, g_post_mix, g_pre_ffn, g_post_ffn,
           w_in, conv_w, conv_b, w_rg_a, b_rg_a, w_rg_i, b_rg_i, lru_lambda, idx_k_ln_g, idx_k_ln_b,
           g_out_rnn, g_out_attn, w_out, w_ffn_gate, w_ffn_up, w_ffn_down):
    depth = w_in.shape[0]
    assert depth == 1, "one layer: prompt/sample caches are not threaded through deeper stacks"
    bp, t_p, d = x_prompt.shape
    assert bp == 1, "the prompt group is one sequence"
    bs, t_s, _ = x_sample.shape
    d_rnn = w_rg_a.shape[1] * w_rg_a.shape[2]
    d_attn = d - d_rnn
    n_heads = d_attn // HEAD_DIM
    gqa = n_heads // N_KV
    d_kv = N_KV * HEAD_DIM
    d_qi = N_IDX_HEADS * D_IDX
    n_pages = page_table.shape[1]
    page = cache_k.shape[2]
    past = n_pages * page
    width = conv_w.shape[1]
    sizes = (d_rnn, d_attn, d_kv, d_qi)

    w_in0 = w_in[0]
    c3 = 2 * d_rnn + d_attn + 2 * d_kv + d_qi
    w = {
        "in_t": jnp.swapaxes(w_in0, 0, 1),
        "kiwi": _pad_cols(w_in0[:, c3:], D_IDX + LANES).astype(BF16),
        "out": w_out[0],
        "ffn_gate": w_ffn_gate[0],
        "ffn_up": w_ffn_up[0],
        "ffn_down": w_ffn_down[0].astype(BF16),
    }

    n_c = _round_up(bs + 1, BF16_ROWS)
    c_all = jnp.concatenate([c_sample, c_prompt, jnp.zeros((n_c - bs - 1, d), F32)], axis=0)
    mod_all = modulation(c_all, w_mod[0], b_mod[0])
    mods_p = [mod_all[bs:bs + 1, j * d:(j + 1) * d] for j in range(6)]
    mods_s = [jnp.repeat(mod_all[:bs, j * d:(j + 1) * d], t_s, axis=0) for j in range(6)]

    rnn_w = (conv_w[0], conv_b[0], w_rg_a[0], b_rg_a[0], w_rg_i[0], b_rg_i[0], lru_lambda[0])
    out_w = (g_out_rnn[0], g_out_attn[0], g_post_mix[0], g_pre_ffn[0], g_post_ffn[0])

    xp = x_prompt.reshape(t_p, d)
    z_rg, q, k, v, qi, ki, wi = _mixer_in(xp, mods_p, g_pre_mix[0], w, idx_k_ln_g[0], idx_k_ln_b[0], sizes)
    y_rnn, h_p = rglru_prompt(z_rg, *rnn_w)
    n_sel_p = min(TOPK_MAX, t_p // 4)
    v_heads = v.astype(BF16).reshape(t_p, N_KV, HEAD_DIM)
    v_ones = jnp.ones((t_p, N_KV, VT_ROWS - HEAD_DIM), BF16)
    v_t = jnp.concatenate([v_heads, v_ones], axis=2).reshape(t_p, N_KV * VT_ROWS).T
    y_attn = prompt_attention(q, qi, wi.T, ki.astype(BF16), k.astype(BF16), v_t,
                              _prompt_bias(rel_bias, gqa), n_sel_p)
    y_p = _block_out(xp, y_rnn, y_attn, mods_p, w, *out_w)
    k_prompt = k.reshape(1, 1, t_p, N_KV, HEAD_DIM)
    v_prompt = v.reshape(1, 1, t_p, N_KV, HEAD_DIM)
    kidx_prompt = ki.reshape(1, 1, t_p, D_IDX)
    conv_prompt = z_rg[t_p - (width - 1):, :d_rnn].reshape(1, 1, width - 1, d_rnn)
    h_prompt = h_p.reshape(1, 1, d_rnn)

    m_s = bs * t_s
    xs = x_sample.reshape(m_s, d)
    z_rg, q, k, v, qi, ki, wi = _mixer_in(xs, mods_s, g_pre_mix[0], w, idx_k_ln_g[0], idx_k_ln_b[0], sizes)
    z_t = z_rg.reshape(bs, t_s, 2 * d_rnn).transpose(1, 0, 2)
    y_t, h_s = rglru_sample(z_t, state_conv[0].transpose(1, 0, 2), state_h[0], *rnn_w)
    y_rnn = y_t.transpose(1, 0, 2).reshape(m_s, d_rnn)

    def head_major(a, n):
        return a.reshape(bs, t_s, n, a.shape[1] // n).transpose(0, 2, 1, 3).reshape(bs, n * t_s, -1)

    n_new = _round_up(t_s, BF16_ROWS)

    def new_rows(a):
        a = a.reshape(bs, t_s, -1).astype(BF16)
        return jnp.pad(a, ((0, 0), (0, n_new - t_s), (0, 0)))

    lp = past + _round_up(t_s, LANES)
    n_sel_s = min(TOPK_MAX, (past + t_s) // 4)
    wi_rows = wi.reshape(bs, t_s, N_IDX_HEADS).transpose(0, 2, 1).reshape(bs, N_IDX_HEADS * t_s, 1)
    y_attn = sample_attention(
        page_table, head_major(qi, N_IDX_HEADS), wi_rows, head_major(q, n_heads),
        new_rows(ki), new_rows(k), new_rows(v), _sample_bias(rel_bias, past, t_s, lp),
        cache_idx_k[0], cache_k[0].reshape(-1, page * N_KV, HEAD_DIM),
        cache_v[0].reshape(-1, page * N_KV, HEAD_DIM),
        t_s, n_sel_s).reshape(m_s, d_attn)
    y_s = _block_out(xs, y_rnn, y_attn, mods_s, w, *out_w)
    k_sample = k.reshape(1, bs, t_s, N_KV, HEAD_DIM)
    v_sample = v.reshape(1, bs, t_s, N_KV, HEAD_DIM)
    kidx_sample = ki.reshape(1, bs, t_s, D_IDX)
    xr_s = z_rg[:, :d_rnn].reshape(bs, t_s, d_rnn)
    hist = jnp.concatenate([state_conv[0], xr_s], axis=1)[:, t_s:]
    conv_sample = hist.reshape(1, bs, width - 1, d_rnn)
    h_sample = h_s.reshape(1, bs, d_rnn)

    return (y_p.reshape(1, t_p, d), y_s.reshape(bs, t_s, d), k_prompt, v_prompt, kidx_prompt,
            conv_prompt, h_prompt, k_sample, v_sample, kidx_sample, conv_sample, h_sample)
```

```python
import functools
import math

import jax
import jax.numpy as jnp
import numpy as np
from jax import lax
from jax.experimental import pallas as pl
from jax.experimental.pallas import tpu as pltpu

HEAD_DIM = 128
N_KV = 4
N_IDX_HEADS = 16
D_IDX = 128
TOPK_MAX = 256
MAX_DISTANCE = 128
LRU_C = 8.0
EPS = 1e-6
NEG = -1e30

LANES = 128
SUBLANES = 8
BF16_ROWS = 16
VMEM_LIMIT_BYTES = 56 * 1024 * 1024
ROWWISE_TILE = 128

F32 = jnp.float32
BF16 = jnp.bfloat16
I32 = jnp.int32
INT_MIN = -(2 ** 31)
NT_DIMS = (((1,), (1,)), ((), ()))


def _sortable_key_of(x):
    b = int(np.array(x, np.float32).view(np.int32))
    return b ^ ((b >> 31) & 0x7FFFFFFF)


NEG_KEY = _sortable_key_of(NEG)


def _params(*semantics):
    return pltpu.CompilerParams(dimension_semantics=semantics, vmem_limit_bytes=VMEM_LIMIT_BYTES)


def _round_up(x, m):
    return -(-x // m) * m


def _row_tile(m, cap):
    t = min(m, cap)
    while m % t:
        t //= 2
    return t


def _sortable_keys(x):
    bits = pltpu.bitcast(x, I32)
    return bits ^ ((bits >> 31) & 0x7FFFFFFF)


def _rms(x, g):
    return x * lax.rsqrt(jnp.mean(x * x, axis=-1, keepdims=True) + EPS) * g


def _mod_kernel(c_ref, w_ref, b_ref, o_ref, a_s):
    @pl.when(pl.program_id(0) == 0)
    def _():
        c = c_ref[...]
        a_s[...] = (c * jax.nn.sigmoid(c)).astype(BF16)

    acc = jnp.dot(a_s[...], w_ref[...].astype(BF16), preferred_element_type=F32)
    o_ref[...] = acc + b_ref[...]


def modulation(c, w_mod, b_mod):
    b, d = c.shape
    n = w_mod.shape[1]
    tn = 512
    return pl.pallas_call(
        _mod_kernel,
        grid=(n // tn,),
        in_specs=[pl.BlockSpec((b, d), lambda j: (0, 0)),
                  pl.BlockSpec((d, tn), lambda j: (0, j)),
                  pl.BlockSpec((1, tn), lambda j: (0, j))],
        out_specs=pl.BlockSpec((b, tn), lambda j: (0, j)),
        out_shape=jax.ShapeDtypeStruct((b, n), F32),
        scratch_shapes=[pltpu.VMEM((b, d), BF16)],
        compiler_params=_params("arbitrary"),
        name="modulation",
    )(c, w_mod, b_mod.reshape(1, n))


def _mod_spec(mod, tm, d):
    if mod.shape[0] == 1:
        return pl.BlockSpec((1, d), lambda i: (0, 0))
    return pl.BlockSpec((tm, d), lambda i: (i, 0))


def _prenorm_kernel(x_ref, g_ref, sc_ref, sh_ref, o_ref):
    y = _rms(x_ref[...], g_ref[...])
    o_ref[...] = (y * (1.0 + sc_ref[...]) + sh_ref[...]).astype(o_ref.dtype)


def prenorm(x, g, scale, shift):
    m, d = x.shape
    tm = _row_tile(m, ROWWISE_TILE * (4 if scale.shape[0] == 1 else 1))
    return pl.pallas_call(
        _prenorm_kernel,
        grid=(m // tm,),
        in_specs=[pl.BlockSpec((tm, d), lambda i: (i, 0)),
                  pl.BlockSpec((1, d), lambda i: (0, 0)),
                  _mod_spec(scale, tm, d), _mod_spec(shift, tm, d)],
        out_specs=pl.BlockSpec((tm, d), lambda i: (i, 0)),
        out_shape=jax.ShapeDtypeStruct((m, d), BF16),
        compiler_params=_params("parallel"),
        name="prenorm",
    )(x, g.reshape(1, d), scale, shift)


def _group_norm_kernel(a_ref, b_ref, ga_ref, gb_ref, o_ref):
    da = a_ref.shape[1]
    o_ref[:, :da] = _rms(a_ref[...], ga_ref[...]).astype(o_ref.dtype)
    o_ref[:, da:] = _rms(b_ref[...], gb_ref[...]).astype(o_ref.dtype)


def group_norm_concat(a, b, ga, gb):
    m, da = a.shape
    db = b.shape[1]
    tm = _row_tile(m, ROWWISE_TILE * 4)
    return pl.pallas_call(
        _group_norm_kernel,
        grid=(m // tm,),
        in_specs=[pl.BlockSpec((tm, da), lambda i: (i, 0)),
                  pl.BlockSpec((tm, db), lambda i: (i, 0)),
                  pl.BlockSpec((1, da), lambda i: (0, 0)),
                  pl.BlockSpec((1, db), lambda i: (0, 0))],
        out_specs=pl.BlockSpec((tm, da + db), lambda i: (i, 0)),
        out_shape=jax.ShapeDtypeStruct((m, da + db), BF16),
        compiler_params=_params("parallel"),
        name="group_norm_concat",
    )(a, b, ga.reshape(1, da), gb.reshape(1, db))


def _mid_kernel(x_ref, mix_ref, gpost_ref, gate_ref, gpre_ref, sc_ref, sh_ref, x1_ref, h_ref):
    x1 = x_ref[...] + gate_ref[...] * _rms(mix_ref[...], gpost_ref[...])
    x1_ref[...] = x1
    h_ref[...] = (_rms(x1, gpre_ref[...]) * (1.0 + sc_ref[...]) + sh_ref[...]).astype(h_ref.dtype)


def mid_block(x, mixed, g_post, gate1, g_pre, scale2, shift2):
    m, d = x.shape
    tm = _row_tile(m, ROWWISE_TILE)
    row = pl.BlockSpec((tm, d), lambda i: (i, 0))
    vec = pl.BlockSpec((1, d), lambda i: (0, 0))
    return pl.pallas_call(
        _mid_kernel,
        grid=(m // tm,),
        in_specs=[row, row, vec, _mod_spec(gate1, tm, d), vec,
                  _mod_spec(scale2, tm, d), _mod_spec(shift2, tm, d)],
        out_specs=[row, row],
        out_shape=[jax.ShapeDtypeStruct((m, d), F32), jax.ShapeDtypeStruct((m, d), BF16)],
        compiler_params=_params("parallel"),
        name="mid_block",
    )(x, mixed, g_post.reshape(1, d), gate1, g_pre.reshape(1, d), scale2, shift2)


def _final_kernel(x_ref, f_ref, g_ref, gate_ref, o_ref):
    o_ref[...] = x_ref[...] + gate_ref[...] * _rms(f_ref[...], g_ref[...])


def final_block(x1, f, g_post, gate2):
    m, d = x1.shape
    tm = _row_tile(m, ROWWISE_TILE * 2)
    row = pl.BlockSpec((tm, d), lambda i: (i, 0))
    return pl.pallas_call(
        _final_kernel,
        grid=(m // tm,),
        in_specs=[row, row, pl.BlockSpec((1, d), lambda i: (0, 0)), _mod_spec(gate2, tm, d)],
        out_specs=row,
        out_shape=jax.ShapeDtypeStruct((m, d), F32),
        compiler_params=_params("parallel"),
        name="final_block",
    )(x1, f, g_post.reshape(1, d), gate2)


def _mm_w32_kernel(a_ref, w_ref, o_ref, wbf_s, *, scale, w_is_transposed):
    @pl.when(pl.program_id(1) == 0)
    def _():
        wbf_s[...] = w_ref[...].astype(BF16)

    if w_is_transposed:
        acc = lax.dot_general(a_ref[...], wbf_s[...], NT_DIMS, preferred_element_type=F32)
    else:
        acc = jnp.dot(a_ref[...], wbf_s[...], preferred_element_type=F32)
    if scale != 1.0:
        acc = acc * scale
    o_ref[...] = acc.astype(o_ref.dtype)


def matmul_w32(a, w, col0, n, out_dtype, *, scale=1.0, w_is_transposed=False, name="matmul_w32"):
    m, k = a.shape
    tm = _row_tile(m, 1024)
    tn = _row_tile(n, 512)
    assert col0 % tn == 0, (col0, tn)
    j0 = col0 // tn
    if w_is_transposed:
        w_block = (tn, k)
        w_spec = pl.BlockSpec(w_block, lambda j, i: (j0 + j, 0))
    else:
        w_block = (k, tn)
        w_spec = pl.BlockSpec(w_block, lambda j, i: (0, j0 + j))
    return pl.pallas_call(
        functools.partial(_mm_w32_kernel, scale=scale, w_is_transposed=w_is_transposed),
        grid=(n // tn, m // tm),
        in_specs=[pl.BlockSpec((tm, k), lambda j, i: (i, 0)), w_spec],
        out_specs=pl.BlockSpec((tm, tn), lambda j, i: (i, j)),
        out_shape=jax.ShapeDtypeStruct((m, n), out_dtype),
        scratch_shapes=[pltpu.VMEM(w_block, BF16)],
        compiler_params=_params("parallel", "arbitrary"),
        name=name,
    )(a, w)


def _mm_acc_kernel(a_ref, w_ref, o_ref, acc_s):
    kk = pl.program_id(2)

    @pl.when(kk == 0)
    def _():
        acc_s[...] = jnp.zeros_like(acc_s)

    acc_s[...] += jnp.dot(a_ref[...], w_ref[...], preferred_element_type=F32)

    @pl.when(kk == pl.num_programs(2) - 1)
    def _():
        o_ref[...] = acc_s[...]


def matmul_ksplit(a, w, *, k_steps, name="matmul_ksplit"):
    m, k = a.shape
    n = w.shape[1]
    tm = _row_tile(m, 1024)
    tn = _row_tile(n, 512)
    tk = k // k_steps
    return pl.pallas_call(
        _mm_acc_kernel,
        grid=(m // tm, n // tn, k_steps),
        in_specs=[pl.BlockSpec((tm, tk), lambda i, j, s: (i, s)),
                  pl.BlockSpec((tk, tn), lambda i, j, s: (s, j))],
        out_specs=pl.BlockSpec((tm, tn), lambda i, j, s: (i, j)),
        out_shape=jax.ShapeDtypeStruct((m, n), F32),
        scratch_shapes=[pltpu.VMEM((tm, tn), F32)],
        compiler_params=_params("parallel", "arbitrary", "arbitrary"),
        name=name,
    )(a, w)


def _swiglu_kernel(a_ref, wg_ref, wu_ref, o_ref, wg_s, wu_s):
    @pl.when(pl.program_id(1) == 0)
    def _():
        wg_s[...] = wg_ref[...].astype(BF16)
        wu_s[...] = wu_ref[...].astype(BF16)

    a = a_ref[...]
    g = jnp.dot(a, wg_s[...], preferred_element_type=F32)
    u = jnp.dot(a, wu_s[...], preferred_element_type=F32)
    o_ref[...] = (g * jax.nn.sigmoid(g) * u).astype(o_ref.dtype)


def swiglu(a, wg, wu):
    m, k = a.shape
    n = wg.shape[1]
    tm = _row_tile(m, 1024)
    tn = _row_tile(n, 256)
    wspec = pl.BlockSpec((k, tn), lambda j, i: (0, j))
    return pl.pallas_call(
        _swiglu_kernel,
        grid=(n // tn, m // tm),
        in_specs=[pl.BlockSpec((tm, k), lambda j, i: (i, 0)), wspec, wspec],
        out_specs=pl.BlockSpec((tm, tn), lambda j, i: (i, j)),
        out_shape=jax.ShapeDtypeStruct((m, n), BF16),
        scratch_shapes=[pltpu.VMEM((k, tn), BF16), pltpu.VMEM((k, tn), BF16)],
        compiler_params=_params("parallel", "arbitrary"),
        name="swiglu",
    )(a, wg, wu)


def _idxkey_kernel(a_ref, w_ref, g_ref, b_ref, ki_ref, wi_ref, *, wi_scale):
    z = jnp.dot(a_ref[...], w_ref[...], preferred_element_type=F32)
    ki = z[:, :D_IDX]
    mu = jnp.mean(ki, axis=-1, keepdims=True)
    var = jnp.mean(jnp.square(ki - mu), axis=-1, keepdims=True)
    ki_ref[...] = (ki - mu) * lax.rsqrt(var + EPS) * g_ref[...] + b_ref[...]
    wi_ref[...] = z[:, D_IDX:] * wi_scale


def indexer_key_proj(a, w, ln_g, ln_b, wi_scale):
    m, k = a.shape
    n = w.shape[1]
    tm = _row_tile(m, 512)
    return pl.pallas_call(
        functools.partial(_idxkey_kernel, wi_scale=wi_scale),
        grid=(m // tm,),
        in_specs=[pl.BlockSpec((tm, k), lambda i: (i, 0)),
                  pl.BlockSpec((k, n), lambda i: (0, 0)),
                  pl.BlockSpec((1, D_IDX), lambda i: (0, 0)),
                  pl.BlockSpec((1, D_IDX), lambda i: (0, 0))],
        out_specs=[pl.BlockSpec((tm, D_IDX), lambda i: (i, 0)),
                   pl.BlockSpec((tm, n - D_IDX), lambda i: (i, 0))],
        out_shape=[jax.ShapeDtypeStruct((m, D_IDX), F32),
                   jax.ShapeDtypeStruct((m, n - D_IDX), F32)],
        compiler_params=_params("parallel"),
        name="indexer_key_proj",
    )(a, w, ln_g.reshape(1, D_IDX), ln_b.reshape(1, D_IDX))


def _rg_gates(conv, wa_ref, ba_ref, wi_ref, bi_ref, lam_ref):
    cb = conv.astype(BF16)
    r = jax.nn.sigmoid(jnp.dot(cb, wa_ref[0], preferred_element_type=F32) + ba_ref[...])
    i = jax.nn.sigmoid(jnp.dot(cb, wi_ref[0], preferred_element_type=F32) + bi_ref[...])
    nl = -lam_ref[...]
    softplus = jnp.maximum(nl, 0.0) + jnp.log1p(jnp.exp(-jnp.abs(nl)))
    log_a = -LRU_C * r * softplus
    a = jnp.exp(log_a)
    u = jnp.sqrt(-jnp.tanh(log_a) * (a * a + 1.0)) * (i * conv)
    return a, u


def _rglru_prompt_kernel(xr_ref, gt_ref, cw_ref, cb_ref, wa_ref, ba_ref, wi_ref, bi_ref, lam_ref,
                         y_ref, hlast_ref, xbuf, hc):
    tt = xr_ref.shape[0]
    hist = SUBLANES

    @pl.when(pl.program_id(1) == 0)
    def _():
        xbuf[0:hist, :] = jnp.zeros((hist, LANES), F32)
        hc[...] = jnp.zeros_like(hc)

    xbuf[hist:hist + tt, :] = xr_ref[...]
    width = cw_ref.shape[0]
    conv = cb_ref[...] + xbuf[hist - width + 1:hist - width + 1 + tt, :] * cw_ref[0:1, :]
    for j in range(1, width):
        s = hist - width + 1 + j
        conv = conv + xbuf[s:s + tt, :] * cw_ref[j:j + 1, :]
    xbuf[0:hist, :] = xbuf[tt:tt + hist, :]

    a, u = _rg_gates(conv, wa_ref, ba_ref, wi_ref, bi_ref, lam_ref)

    row = lax.broadcasted_iota(I32, (tt, LANES), 0)
    d = 1
    while d < tt:
        keep = row >= d
        a_sh = jnp.where(keep, pltpu.roll(a, d, 0), 1.0)
        u_sh = jnp.where(keep, pltpu.roll(u, d, 0), 0.0)
        u = a * u_sh + u
        a = a * a_sh
        d *= 2
    h = u + a * hc[...]
    hc[...] = h[tt - 1:tt, :]
    hlast_ref[...] = h[tt - 1:tt, :]
    y_ref[...] = h * jax.nn.gelu(gt_ref[...])


def rglru_prompt(z_rg, conv_w, conv_b, w_a, b_a, w_i, b_i, lam):
    t, c2 = z_rg.shape
    c = c2 // 2
    nb = c // LANES
    tt = _row_tile(t, 1024)
    width = conv_w.shape[0]
    vec = pl.BlockSpec((1, LANES), lambda n, i: (0, n))
    wsp = pl.BlockSpec((1, LANES, LANES), lambda n, i: (n, 0, 0))
    return pl.pallas_call(
        _rglru_prompt_kernel,
        grid=(nb, t // tt),
        in_specs=[pl.BlockSpec((tt, LANES), lambda n, i: (i, n)),
                  pl.BlockSpec((tt, LANES), lambda n, i: (i, nb + n)),
                  pl.BlockSpec((width, LANES), lambda n, i: (0, n)),
                  vec, wsp, vec, wsp, vec, vec],
        out_specs=[pl.BlockSpec((tt, LANES), lambda n, i: (i, n)), vec],
        out_shape=[jax.ShapeDtypeStruct((t, c), F32), jax.ShapeDtypeStruct((1, c), F32)],
        scratch_shapes=[pltpu.VMEM((tt + SUBLANES, LANES), F32), pltpu.VMEM((1, LANES), F32)],
        compiler_params=_params("parallel", "arbitrary"),
        name="rglru_prompt",
    )(z_rg, z_rg, conv_w, conv_b.reshape(1, c), w_a.astype(BF16), b_a.reshape(1, c),
      w_i.astype(BF16), b_i.reshape(1, c), lam.reshape(1, c))


def _rglru_sample_kernel(xr_ref, gt_ref, buf_ref, h0_ref, cw_ref, cb_ref, wa_ref, ba_ref, wi_ref,
                         bi_ref, lam_ref, y_ref, hlast_ref, hist, hc):
    width = cw_ref.shape[0]

    @pl.when(pl.program_id(1) == 0)
    def _():
        hist[...] = buf_ref[...]
        hc[...] = h0_ref[...]

    x = xr_ref[0]
    conv = cb_ref[...] + hist[0] * cw_ref[0:1, :]
    for j in range(1, width - 1):
        conv = conv + hist[j] * cw_ref[j:j + 1, :]
    conv = conv + x * cw_ref[width - 1:width, :]
    for j in range(width - 2):
        hist[j] = hist[j + 1]
    hist[width - 2] = x

    a, u = _rg_gates(conv, wa_ref, ba_ref, wi_ref, bi_ref, lam_ref)
    h = a * hc[...] + u
    hc[...] = h
    hlast_ref[...] = h
    y_ref[0] = h * jax.nn.gelu(gt_ref[0])


def rglru_sample(z_rg, buf, h0, conv_w, conv_b, w_a, b_a, w_i, b_i, lam):
    t, b, c2 = z_rg.shape
    c = c2 // 2
    nb = c // LANES
    width = conv_w.shape[0]
    vec = pl.BlockSpec((1, LANES), lambda n, i: (0, n))
    wsp = pl.BlockSpec((1, LANES, LANES), lambda n, i: (n, 0, 0))
    st = pl.BlockSpec((b, LANES), lambda n, i: (0, n))
    return pl.pallas_call(
        _rglru_sample_kernel,
        grid=(nb, t),
        in_specs=[pl.BlockSpec((1, b, LANES), lambda n, i: (i, 0, n)),
                  pl.BlockSpec((1, b, LANES), lambda n, i: (i, 0, nb + n)),
                  pl.BlockSpec((width - 1, b, LANES), lambda n, i: (0, 0, n)),
                  st,
                  pl.BlockSpec((width, LANES), lambda n, i: (0, n)),
                  vec, wsp, vec, wsp, vec, vec],
        out_specs=[pl.BlockSpec((1, b, LANES), lambda n, i: (i, 0, n)), st],
        out_shape=[jax.ShapeDtypeStruct((t, b, c), F32), jax.ShapeDtypeStruct((b, c), F32)],
        scratch_shapes=[pltpu.VMEM((width - 1, b, LANES), F32), pltpu.VMEM((b, LANES), F32)],
        compiler_params=_params("parallel", "arbitrary"),
        name="rglru_sample",
    )(z_rg, z_rg, buf, h0, conv_w, conv_b.reshape(1, c), w_a.astype(BF16), b_a.reshape(1, c),
      w_i.astype(BF16), b_i.reshape(1, c), lam.reshape(1, c))


def _t5_bucket_np(dist, n_buckets):
    max_exact = n_buckets // 2
    d = np.maximum(dist, 1).astype(np.float32)
    large = max_exact + (np.log(d / np.float32(max_exact)) / np.float32(math.log(MAX_DISTANCE / max_exact))
                         * np.float32(n_buckets - max_exact)).astype(np.int32)
    large = np.minimum(large, n_buckets - 1)
    return np.where(dist < max_exact, dist, large).astype(np.int32)


def _bias_kernel(rb_ref, bucket_ref, o_ref, *, n_buckets, far_bucket, scale):
    h = pl.program_id(0)
    bucket = bucket_ref[...]
    shift = rb_ref[far_bucket, h] if far_bucket is not None else 0.0
    acc = jnp.zeros(bucket.shape, F32)
    for b in range(n_buckets):
        acc = jnp.where(bucket == b, (rb_ref[b, h] - shift) * scale, acc)
    o_ref[0] = acc


def bias_tiles(rel_bias, bucket, far_bucket, scale):
    n_buckets, n_heads = rel_bias.shape
    r, c = bucket.shape
    return pl.pallas_call(
        functools.partial(_bias_kernel, n_buckets=n_buckets, far_bucket=far_bucket, scale=scale),
        grid=(n_heads,),
        in_specs=[pl.BlockSpec(memory_space=pltpu.SMEM),
                  pl.BlockSpec((r, c), lambda h: (0, 0))],
        out_specs=pl.BlockSpec((1, r, c), lambda h: (h, 0, 0)),
        out_shape=jax.ShapeDtypeStruct((n_heads, r, c), F32),
        compiler_params=_params("parallel"),
        name="bias_tiles",
    )(rel_bias, jnp.asarray(bucket))


TQ = 128
TK = 256
TK_FULL = 512
LOG2E = math.log2(math.e)
VT_ROWS = HEAD_DIM + BF16_ROWS
NEAR_TILE_OFFSETS = (-TQ, 0, -TK)


def _pattn_kernel(qi_ref, wit_ref, q_ref, kidx_ref, k_ref, vt_ref, bias_ref, o_ref,
                  keys_s, qis_s, qs_s, s0_s, s1_s, acc_s, m_s, last_tie_s, *, seq_len, n_sel, gqa):
    i = pl.program_id(0)
    pos_bits = seq_len.bit_length()
    n_idx = qis_s.shape[0] // TQ

    for h in range(n_idx):
        qis_s[h * TQ:(h + 1) * TQ, :] = qi_ref[:, h * D_IDX:(h + 1) * D_IDX]
    for kv in range(N_KV):
        for g in range(gqa):
            hh = kv * gqa + g
            qs_s[kv, g * TQ:(g + 1) * TQ, :] = q_ref[:, hh * HEAD_DIM:(hh + 1) * HEAD_DIM]

    qpos = i * TQ + lax.broadcasted_iota(I32, (1, TQ), 1)

    fulls_per_tile = TK_FULL // TQ
    n_full = i // fulls_per_tile
    n_part = (i % fulls_per_tile) // (TK // TQ) + 1
    part0 = n_full * TK_FULL

    def score_tile(k0, tk, masked):
        kt = kidx_ref[pl.ds(k0, tk), :]
        acc = jnp.zeros((tk, TQ), F32)
        for hp in range(n_idx // 2):
            s = lax.dot_general(kt, qis_s[hp * 2 * TQ:(hp + 1) * 2 * TQ, :], NT_DIMS,
                                preferred_element_type=F32)
            for e in range(2):
                h = 2 * hp + e
                acc = acc + jnp.maximum(s[:, e * TQ:(e + 1) * TQ], 0.0) * wit_ref[h:h + 1, :]
        if masked:
            kpos = k0 + lax.broadcasted_iota(I32, (tk, 1), 0)
            acc = jnp.where(kpos <= qpos, acc, NEG)
        keys_s[pl.ds(k0, tk), :] = _sortable_keys(acc)

    def full_scores(j, c):
        score_tile(pl.multiple_of(j * TK_FULL, TK_FULL), TK_FULL, False)
        return c

    def part_scores(j, c):
        score_tile(pl.multiple_of(part0 + j * TK, TK), TK, True)
        return c

    lax.fori_loop(0, n_full, full_scores, 0)
    lax.fori_loop(0, n_part, part_scores, 0)

    n_after = seq_len - (part0 + n_part * TK)

    def count_ge(k0, rows, cand, cnt):
        slab = keys_s[pl.ds(k0, rows), :]
        for c in range(rows // TQ):
            u = slab[c * TQ:(c + 1) * TQ, :]
            ge = u.reshape(TQ // SUBLANES, SUBLANES, TQ) >= cand[None]
            cnt = cnt + jnp.sum(jnp.where(ge, 1, 0), axis=0)
        return cnt

    def over_tiles(tile_fn, carry):
        carry = lax.fori_loop(
            0, n_full, lambda j, c: tile_fn(pl.multiple_of(j * TK_FULL, TK_FULL), TK_FULL, c), carry)
        return lax.fori_loop(
            0, n_part, lambda j, c: tile_fn(pl.multiple_of(part0 + j * TK, TK), TK, c), carry)

    zero_cnt = jnp.zeros((SUBLANES, TQ), I32)

    def bit_step(b, carry):
        thr, n_ge = carry
        cand = thr ^ lax.shift_left(jnp.int32(1), 31 - b)
        cnt = over_tiles(lambda k0, rows, cnt: count_ge(k0, rows, cand, cnt), zero_cnt)
        tot = jnp.sum(cnt, axis=0, keepdims=True)
        tot = tot + jnp.where(cand[0:1] <= NEG_KEY, n_after, 0)
        take = tot >= n_sel
        return jnp.where(take, cand, thr), jnp.where(take, tot, n_ge)

    thr, n_ge = lax.fori_loop(
        0, 32, bit_step,
        (jnp.full((SUBLANES, TQ), INT_MIN, I32), jnp.full((SUBLANES, TQ), seq_len, I32)))
    thr = jnp.maximum(thr[0:1], INT_MIN + 1)

    tied = (n_ge[0:1] > n_sel) & (thr > NEG_KEY)
    last_tie_s[...] = jnp.full((1, TQ), seq_len, I32)

    @pl.when(jnp.max(jnp.where(tied, 1, 0)) > 0)
    def _():
        def count_where(pred, k0, rows, cnt):
            slab = keys_s[pl.ds(k0, rows), :]
            for c in range(rows // TQ):
                u = slab[c * TQ:(c + 1) * TQ, :]
                kpos = k0 + c * TQ + lax.broadcasted_iota(I32, (TQ, 1), 0)
                hit = pred(u, kpos)
                cnt = cnt + jnp.sum(hit.reshape(TQ // SUBLANES, SUBLANES, TQ), axis=0)
            return cnt

        def total(pred):
            cnt = over_tiles(lambda k0, rows, cnt: count_where(pred, k0, rows, cnt), zero_cnt)
            return jnp.sum(cnt, axis=0, keepdims=True)

        n_tie = n_sel - total(lambda u, kpos: jnp.where(u > thr, 1, 0))

        def pos_step(b, c):
            cand = c | lax.shift_left(jnp.int32(1), pos_bits - 1 - b)
            before = total(lambda u, kpos: jnp.where(u == thr, jnp.where(kpos < cand, 1, 0), 0))
            return jnp.where(before < n_tie, cand, c)

        last_tie = lax.fori_loop(0, pos_bits, pos_step, jnp.zeros((1, TQ), I32))
        last_tie_s[...] = jnp.where(tied, last_tie, seq_len)

    last_tie = last_tie_s[...]

    m_s[...] = jnp.full(m_s.shape, NEG, F32)
    acc_s[...] = jnp.zeros(acc_s.shape, F32)

    def tile_start(j):
        return j * TK if isinstance(j, int) else pl.multiple_of(j * TK, TK)

    def tile_mask(k0, near):
        u = keys_s[pl.ds(k0, TK), :]
        kpos = k0 + lax.broadcasted_iota(I32, (TK, 1), 0)
        if near:
            u = jnp.where(kpos <= qpos, u, INT_MIN)
        mask = jnp.where(u >= jnp.where(kpos <= last_tie, thr, thr + 1), 0.0, -jnp.inf)
        return jnp.concatenate([mask] * gqa, axis=1)

    def logits_kv(k0, s_ref, kv, mask, near_kind):
        kt = k_ref[pl.ds(k0, TK), kv * HEAD_DIM:(kv + 1) * HEAD_DIM]
        s = lax.dot_general(kt, qs_s[kv], NT_DIMS, preferred_element_type=F32)
        if near_kind is not None:
            s = s + bias_ref[near_kind, kv]
        s_ref[kv] = s + mask

    def softmax_kv(k0, s_ref, kv):
        m_old = m_s[kv]
        m_new = jnp.maximum(m_old, jnp.max(s_ref[kv], axis=0, keepdims=True))
        alpha = jnp.exp2(m_old - m_new)
        p = jnp.exp2(s_ref[kv] - m_new)
        vt = vt_ref[kv * VT_ROWS:(kv + 1) * VT_ROWS, pl.ds(k0, TK)]
        pv = jnp.dot(vt, p.astype(BF16), preferred_element_type=F32)
        acc_s[kv] = acc_s[kv] * alpha + pv
        m_s[kv] = m_new

    def logits_stage(j, s_ref, near_kind):
        k0 = tile_start(j)
        mask = tile_mask(k0, near_kind is not None)
        for kv in range(N_KV):
            logits_kv(k0, s_ref, kv, mask, near_kind)

    def softmax_stage(j, s_ref):
        k0 = tile_start(j)
        for kv in range(N_KV):
            softmax_kv(k0, s_ref, kv)

    def far_step(j_cur, s_cur, j_next, s_next):
        k0_cur, k0_next = tile_start(j_cur), tile_start(j_next)
        mask = tile_mask(k0_next, False)
        for kv in range(N_KV):
            logits_kv(k0_next, s_next, kv, mask, None)
            softmax_kv(k0_cur, s_cur, kv)

    n_tiles = (i + 2) // 2
    n_far = jnp.maximum(i - 1, 0) // 2
    last_far = jnp.maximum(n_far - 1, 0)
    logits_stage(0, s0_s, None)

    def far_pair(p, c):
        far_step(2 * p, s0_s, 2 * p + 1, s1_s)
        far_step(2 * p + 1, s1_s, jnp.minimum(2 * p + 2, last_far), s0_s)
        return c

    lax.fori_loop(0, n_far // 2, far_pair, 0)

    @pl.when(n_far % 2 == 1)
    def _():
        softmax_stage(n_far - 1, s0_s)

    def near_tile(j, c):
        kind = jnp.where(i % 2 == 1, 0, n_tiles - j)
        logits_stage(j, s0_s, kind)
        softmax_stage(j, s0_s)
        return c

    lax.fori_loop(n_far, n_tiles, near_tile, 0)

    for kv in range(N_KV):
        ot = acc_s[kv, :HEAD_DIM, :] * (1.0 / acc_s[kv, HEAD_DIM:HEAD_DIM + 1, :])
        for g in range(gqa):
            hh = kv * gqa + g
            o_ref[:, hh * HEAD_DIM:(hh + 1) * HEAD_DIM] = ot[:, g * TQ:(g + 1) * TQ].T


def prompt_attention(q, qi, wi_t, kidx, k, v_t, bias, n_sel):
    t, dq = q.shape
    n_heads = dq // HEAD_DIM
    gqa = n_heads // N_KV
    n_idx = qi.shape[1] // D_IDX
    dkv = N_KV * HEAD_DIM
    whole = lambda shape: pl.BlockSpec(shape, lambda i: (0,) * len(shape))
    return pl.pallas_call(
        functools.partial(_pattn_kernel, seq_len=t, n_sel=n_sel, gqa=gqa),
        grid=(t // TQ,),
        in_specs=[pl.BlockSpec((TQ, n_idx * D_IDX), lambda i: (i, 0)),
                  pl.BlockSpec((n_idx, TQ), lambda i: (0, i)),
                  pl.BlockSpec((TQ, dq), lambda i: (i, 0)),
                  whole((t, D_IDX)), whole((t, dkv)), whole((N_KV * VT_ROWS, t)),
                  whole(bias.shape)],
        out_specs=pl.BlockSpec((TQ, dq), lambda i: (i, 0)),
        out_shape=jax.ShapeDtypeStruct((t, dq), F32),
        scratch_shapes=[pltpu.VMEM((t, TQ), I32),
                        pltpu.VMEM((n_idx * TQ, D_IDX), BF16),
                        pltpu.VMEM((N_KV, gqa * TQ, HEAD_DIM), BF16),
                        pltpu.VMEM((N_KV, TK, gqa * TQ), F32),
                        pltpu.VMEM((N_KV, TK, gqa * TQ), F32),
                        pltpu.VMEM((N_KV, VT_ROWS, gqa * TQ), F32),
                        pltpu.VMEM((N_KV, 1, gqa * TQ), F32),
                        pltpu.VMEM((1, TQ), I32)],
        compiler_params=_params("parallel"),
        name="prompt_attention",
    )(qi, wi_t, q, kidx, k, v_t, bias)


RADIX_BITS = 4

def _sattn_kernel(pt_ref, qi_ref, wi_ref, q_ref, kin_ref, kn_ref, vn_ref, bias_ref, *rest,
                  n_pages, page, n_sel, gqa, resolve_ties):
    ki_pages = rest[:n_pages]
    k_pages = rest[n_pages:2 * n_pages]
    v_pages = rest[2 * n_pages:3 * n_pages]
    o_ref, tied_ref, kibuf, kbuf, vbuf = rest[3 * n_pages:]
    del pt_ref
    past = n_pages * page
    lp = kibuf.shape[0]
    t_new = o_ref.shape[1]
    n_new = kin_ref.shape[1]
    rows = q_ref.shape[1]
    dkv = N_KV * HEAD_DIM

    for p in range(n_pages):
        kibuf[p * page:(p + 1) * page, :] = ki_pages[p][0].astype(BF16)
        for kv in range(N_KV):
            rows_kv = pl.ds(kv, page, stride=N_KV)
            cols_kv = slice(kv * HEAD_DIM, (kv + 1) * HEAD_DIM)
            kbuf[p * page:(p + 1) * page, cols_kv] = k_pages[p][0, rows_kv, :].astype(BF16)
            vbuf[p * page:(p + 1) * page, cols_kv] = v_pages[p][0, rows_kv, :].astype(BF16)
    kibuf[past:, :] = jnp.zeros((lp - past, D_IDX), BF16)
    kbuf[past:, :] = jnp.zeros((lp - past, dkv), BF16)
    vbuf[past:, :] = jnp.zeros((lp - past, dkv), BF16)
    kibuf[past:past + n_new, :] = kin_ref[0]
    kbuf[past:past + n_new, :] = kn_ref[0]
    vbuf[past:past + n_new, :] = vn_ref[0]

    kpos = lax.broadcasted_iota(I32, (1, lp), 1)
    qpos = past + lax.broadcasted_iota(I32, (t_new, 1), 0)
    visible = kpos <= qpos

    s = lax.dot_general(qi_ref[0], kibuf[...], NT_DIMS, preferred_element_type=F32)
    s = jnp.maximum(s, 0.0) * wi_ref[0]
    n_idx = s.shape[0] // t_new
    score = jnp.sum(s.reshape(n_idx, t_new, lp), axis=0)
    score = jnp.where(visible, score, NEG)
    keys = jnp.where(kpos < past + t_new, _sortable_keys(score), INT_MIN)

    prefix = jnp.zeros((t_new, 1), I32)
    for shift in range(32 - RADIX_BITS, -1, -RADIX_BITS):
        digit = jnp.zeros((t_new, 1), I32)
        for v in range(1, 2 ** RADIX_BITS):
            step = int(np.array((v << shift) & 0xFFFFFFFF, np.uint32).view(np.int32))
            cand = (prefix | step) ^ INT_MIN
            tot = jnp.sum(jnp.where(keys >= cand, 1, 0), axis=1, keepdims=True)
            digit = digit + jnp.where(tot >= n_sel, 1, 0)
        prefix = prefix | lax.shift_left(digit, jnp.int32(shift))
    thr = jnp.maximum(prefix ^ INT_MIN, INT_MIN + 1)

    n_ge = jnp.sum(jnp.where(keys >= thr, 1, 0), axis=1, keepdims=True)
    tied = (n_ge > n_sel) & (thr > NEG_KEY)
    tied_ref[0] = jnp.where(tied, 1, 0)
    if resolve_ties:
        pos_bits = lp.bit_length()
        n_tie = n_sel - jnp.sum(jnp.where(keys > thr, 1, 0), axis=1, keepdims=True)

        def pos_step(b, c):
            cand = c | lax.shift_left(jnp.int32(1), pos_bits - 1 - b)
            before = jnp.sum(jnp.where(keys == thr, jnp.where(kpos < cand, 1, 0), 0), axis=1, keepdims=True)
            return jnp.where(before < n_tie, cand, c)

        last_tie = lax.fori_loop(0, pos_bits, pos_step, jnp.zeros((t_new, 1), I32))
        thr = jnp.where(kpos <= jnp.where(tied, last_tie, lp), thr, thr + 1)
    sel = jnp.where(visible, keys, INT_MIN) >= thr

    qrep = jnp.concatenate([q_ref[0]] * N_KV, axis=1)
    rkv = lax.broadcasted_iota(I32, (rows, dkv), 0) // (gqa * t_new)
    ckv = lax.broadcasted_iota(I32, (rows, dkv), 1) // HEAD_DIM
    qblk = jnp.where(rkv == ckv, qrep, jnp.zeros_like(qrep))
    logits = lax.dot_general(qblk, kbuf[...], NT_DIMS, preferred_element_type=F32) + bias_ref[...]
    n_heads = rows // t_new
    logits = jnp.where(sel[None], logits.reshape(n_heads, t_new, lp), -jnp.inf)
    m = jnp.maximum(jnp.max(logits, axis=-1, keepdims=True), NEG)
    p = jnp.exp2(logits - m)
    l = jnp.sum(p, axis=-1, keepdims=True)
    o = jnp.dot(p.reshape(rows, lp).astype(BF16), vbuf[...], preferred_element_type=F32)
    o = o.reshape(n_heads, t_new, dkv) * (1.0 / l)
    for kv in range(N_KV):
        for g in range(gqa):
            hh = kv * gqa + g
            o_ref[0, :, hh * HEAD_DIM:(hh + 1) * HEAD_DIM] = o[hh, :, kv * HEAD_DIM:(kv + 1) * HEAD_DIM]


def sample_attention(page_table, qi, wi, q, ki_new, k_new, v_new, bias, cache_ki, cache_k, cache_v,
                     t_new, n_sel):
    b, rows, _ = q.shape
    n_heads = rows // t_new
    gqa = n_heads // N_KV
    n_pages = page_table.shape[1]
    page = cache_ki.shape[1]
    dkv = N_KV * HEAD_DIM
    lp = bias.shape[1]
    n_new = ki_new.shape[1]
    per_seq = lambda shape: pl.BlockSpec((1,) + shape, lambda s, pt: (s, 0, 0))

    def page_spec(page_rows, p):
        return pl.BlockSpec((1, page_rows, HEAD_DIM), lambda s, pt: (pt[s, p], 0, 0))

    in_specs = [per_seq((qi.shape[1], D_IDX)), per_seq((wi.shape[1], 1)), per_seq((rows, HEAD_DIM)),
                per_seq((n_new, D_IDX)), per_seq((n_new, dkv)), per_seq((n_new, dkv)),
                pl.BlockSpec((rows, lp), lambda s, pt: (0, 0))]
    in_specs += [page_spec(page, p) for p in range(n_pages)]
    in_specs += [page_spec(page * N_KV, p) for p in range(n_pages)]
    in_specs += [page_spec(page * N_KV, p) for p in range(n_pages)]
    grid_spec = pltpu.PrefetchScalarGridSpec(
        num_scalar_prefetch=1,
        grid=(b,),
        in_specs=in_specs,
        out_specs=[pl.BlockSpec((1, t_new, n_heads * HEAD_DIM), lambda s, pt: (s, 0, 0)),
                   pl.BlockSpec((1, t_new, 1), lambda s, pt: (s, 0, 0))],
        scratch_shapes=[pltpu.VMEM((lp, D_IDX), BF16), pltpu.VMEM((lp, dkv), BF16),
                        pltpu.VMEM((lp, dkv), BF16)],
    )
    operands = (page_table, qi, wi, q, ki_new, k_new, v_new, bias,
                *([cache_ki] * n_pages), *([cache_k] * n_pages), *([cache_v] * n_pages))

    def run(resolve_ties):
        return pl.pallas_call(
            functools.partial(_sattn_kernel, n_pages=n_pages, page=page, n_sel=n_sel, gqa=gqa,
                              resolve_ties=resolve_ties),
            grid_spec=grid_spec,
            out_shape=[jax.ShapeDtypeStruct((b, t_new, n_heads * HEAD_DIM), F32),
                       jax.ShapeDtypeStruct((b, t_new, 1), I32)],
            compiler_params=_params("parallel"),
            name="sample_attention_ties" if resolve_ties else "sample_attention",
        )(*operands)

    out, tied = run(False)
    return lax.cond(jnp.any(tied > 0), lambda: run(True)[0], lambda: out)


def _pad_cols(w, n):
    return jnp.pad(w, ((0, 0), (0, n - w.shape[1])))


def _prompt_bias(rel_bias, gqa):
    n_buckets = rel_bias.shape[0]
    n_kinds = len(NEAR_TILE_OFFSETS)
    kl = np.arange(TK)[:, None]
    ql = np.arange(TQ)[None, :]
    dist = np.stack([np.maximum(ql - (off + kl), 0) for off in NEAR_TILE_OFFSETS])
    bucket = _t5_bucket_np(dist, n_buckets).reshape(n_kinds * TK, TQ)
    far = int(_t5_bucket_np(np.array([MAX_DISTANCE + 1]), n_buckets)[0])
    tiles = bias_tiles(rel_bias, bucket, far, LOG2E)
    tiles = tiles.reshape(N_KV, gqa, n_kinds, TK, TQ).transpose(2, 0, 3, 1, 4)
    return tiles.reshape(n_kinds, N_KV, TK, gqa * TQ)


def _sample_bias(rel_bias, past, t_new, lp):
    n_buckets, n_heads = rel_bias.shape
    dist = np.maximum(past + np.arange(t_new)[:, None] - np.arange(lp)[None, :], 0)
    tiles = bias_tiles(rel_bias, _t5_bucket_np(dist, n_buckets), None, LOG2E)
    return tiles.reshape(n_heads * t_new, lp)


def _mixer_in(x, mods, g_pre_mix, w, ln_g, ln_b, sizes):
    d_rnn, d_q, d_kv, d_qi = sizes
    h1 = prenorm(x, g_pre_mix, mods[1], mods[0])
    c_q = 2 * d_rnn
    c_kv = c_q + d_q
    c_qi = c_kv + 2 * d_kv
    proj = functools.partial(matmul_w32, h1, w["in_t"], w_is_transposed=True)
    z_rg = proj(0, c_q, F32, name="in_proj_rg")
    q = proj(c_q, d_q, BF16, scale=HEAD_DIM ** -0.5 * LOG2E, name="in_proj_q")
    kv = proj(c_kv, 2 * d_kv, F32, name="in_proj_kv")
    qi = proj(c_qi, d_qi, BF16, name="in_proj_qi")
    n_idx = d_qi // D_IDX
    ki, wi = indexer_key_proj(h1, w["kiwi"], ln_g, ln_b, float(d_qi) ** -0.5)
    return z_rg, q, kv[:, :d_kv], kv[:, d_kv:], qi, ki, wi[:, :n_idx]


def _block_out(x, y_rnn, y_attn, mods, w, g_out_rnn, g_out_attn, g_post_mix, g_pre_ffn, g_post_ffn):
    ycat = group_norm_concat(y_rnn, y_attn, g_out_rnn, g_out_attn)
    mixed = matmul_w32(ycat, w["out"], 0, w["out"].shape[1], F32, name="out_proj")
    x1, h2 = mid_block(x, mixed, g_post_mix, mods[2], g_pre_ffn, mods[4], mods[3])
    hf = swiglu(h2, w["ffn_gate"], w["ffn_up"])
    f = matmul_ksplit(hf, w["ffn_down"], k_steps=2, name="ffn_down")
    return final_block(x1, f, g_post_ffn, mods[5])


def kernel(x_prompt, x_sample, cache_k, cache_v, cache_idx_k, state_conv, state_h, page_table,
           c_prompt, c_sample, rel_bias, w_mod, b_mod, g_pre_mix, g_post_mix, g_pre_ffn, g_post_ffn,
           w_in, conv_w, conv_b, w_rg_a, b_rg_a, w_rg_i, b_rg_i, lru_lambda, idx_k_ln_g, idx_k_ln_b,
           g_out_rnn, g_out_attn, w_out, w_ffn_gate, w_ffn_up, w_ffn_down):
    depth = w_in.shape[0]
    assert depth == 1, "one layer: prompt/sample caches are not threaded through deeper stacks"
    bp, t_p, d = x_prompt.shape
    assert bp == 1, "the prompt group is one sequence"
    bs, t_s, _ = x_sample.shape
    d_rnn = w_rg_a.shape[1] * w_rg_a.shape[2]
    d_attn = d - d_rnn
    n_heads = d_attn // HEAD_DIM
    gqa = n_heads // N_KV
    d_kv = N_KV * HEAD_DIM
    d_qi = N_IDX_HEADS * D_IDX
    n_pages = page_table.shape[1]
    page = cache_k.shape[2]
    past = n_pages * page
    width = conv_w.shape[1]
    sizes = (d_rnn, d_attn, d_kv, d_qi)

    w_in0 = w_in[0]
    c3 = 2 * d_rnn + d_attn + 2 * d_kv + d_qi
    w = {
        "in_t": jnp.swapaxes(w_in0, 0, 1),
        "kiwi": _pad_cols(w_in0[:, c3:], D_IDX + LANES).astype(BF16),
        "out": w_out[0],
        "ffn_gate": w_ffn_gate[0],
        "ffn_up": w_ffn_up[0],
        "ffn_down": w_ffn_down[0].astype(BF16),
    }

    n_c = _round_up(bs + 1, BF16_ROWS)
    c_all = jnp.concatenate([c_sample, c_prompt, jnp.zeros((n_c - bs - 1, d), F32)], axis=0)
    mod_all = modulation(c_all, w_mod[0], b_mod[0])
    mods_p = [mod_all[bs:bs + 1, j * d:(j + 1) * d] for j in range(6)]
    mods_s = [jnp.repeat(mod_all[:bs, j * d:(j + 1) * d], t_s, axis=0) for j in range(6)]

    rnn_w = (conv_w[0], conv_b[0], w_rg_a[0], b_rg_a[0], w_rg_i[0], b_rg_i[0], lru_lambda[0])
    out_w = (g_out_rnn[0], g_out_attn[0], g_post_mix[0], g_pre_ffn[0], g_post_ffn[0])

    xp = x_prompt.reshape(t_p, d)
    z_rg, q, k, v, qi, ki, wi = _mixer_in(xp, mods_p, g_pre_mix[0], w, idx_k_ln_g[0], idx_k_ln_b[0], sizes)
    y_rnn, h_p = rglru_prompt(z_rg, *rnn_w)
    n_sel_p = min(TOPK_MAX, t_p // 4)
    v_heads = v.astype(BF16).reshape(t_p, N_KV, HEAD_DIM)
    v_ones = jnp.ones((t_p, N_KV, VT_ROWS - HEAD_DIM), BF16)
    v_t = jnp.concatenate([v_heads, v_ones], axis=2).reshape(t_p, N_KV * VT_ROWS).T
    y_attn = prompt_attention(q, qi, wi.T, ki.astype(BF16), k.astype(BF16), v_t,
                              _prompt_bias(rel_bias, gqa), n_sel_p)
    y_p = _block_out(xp, y_rnn, y_attn, mods_p, w, *out_w)
    k_prompt = k.reshape(1, 1, t_p, N_KV, HEAD_DIM)
    v_prompt = v.reshape(1, 1, t_p, N_KV, HEAD_DIM)
    kidx_prompt = ki.reshape(1, 1, t_p, D_IDX)
    conv_prompt = z_rg[t_p - (width - 1):, :d_rnn].reshape(1, 1, width - 1, d_rnn)
    h_prompt = h_p.reshape(1, 1, d_rnn)

    m_s = bs * t_s
    xs = x_sample.reshape(m_s, d)
    z_rg, q, k, v, qi, ki, wi = _mixer_in(xs, mods_s, g_pre_mix[0], w, idx_k_ln_g[0], idx_k_ln_b[0], sizes)
    z_t = z_rg.reshape(bs, t_s, 2 * d_rnn).transpose(1, 0, 2)
    y_t, h_s = rglru_sample(z_t, state_conv[0].transpose(1, 0, 2), state_h[0], *rnn_w)
    y_rnn = y_t.transpose(1, 0, 2).reshape(m_s, d_rnn)

    def head_major(a, n):
        return a.reshape(bs, t_s, n, a.shape[1] // n).transpose(0, 2, 1, 3).reshape(bs, n * t_s, -1)

    n_new = _round_up(t_s, BF16_ROWS)

    def new_rows(a):
        a = a.reshape(bs, t_s, -1).astype(BF16)
        return jnp.pad(a, ((0, 0), (0, n_new - t_s), (0, 0)))

    lp = past + _round_up(t_s, LANES)
    n_sel_s = min(TOPK_MAX, (past + t_s) // 4)
    wi_rows = wi.reshape(bs, t_s, N_IDX_HEADS).transpose(0, 2, 1).reshape(bs, N_IDX_HEADS * t_s, 1)
    y_attn = sample_attention(
        page_table, head_major(qi, N_IDX_HEADS), wi_rows, head_major(q, n_heads),
        new_rows(ki), new_rows(k), new_rows(v), _sample_bias(rel_bias, past, t_s, lp),
        cache_idx_k[0], cache_k[0].reshape(-1, page * N_KV, HEAD_DIM),
        cache_v[0].reshape(-1, page * N_KV, HEAD_DIM),
        t_s, n_sel_s).reshape(m_s, d_attn)
    y_s = _block_out(xs, y_rnn, y_attn, mods_s, w, *out_w)
    k_sample = k.reshape(1, bs, t_s, N_KV, HEAD_DIM)
    v_sample = v.reshape(1, bs, t_s, N_KV, HEAD_DIM)
    kidx_sample = ki.reshape(1, bs, t_s, D_IDX)
    xr_s = z_rg[:, :d_rnn].reshape(bs, t_s, d_rnn)
    hist = jnp.concatenate([state_conv[0], xr_s], axis=1)[:, t_s:]
    conv_sample = hist.reshape(1, bs, width - 1, d_rnn)
    h_sample = h_s.reshape(1, bs, d_rnn)

    return (y_p.reshape(1, t_p, d), y_s.reshape(bs, t_s, d), k_prompt, v_prompt, kidx_prompt,
            conv_prompt, h_prompt, k_sample, v_sample, kidx_sample, conv_sample, h_sample)
```

```python
import functools
import math

import jax
import jax.numpy as jnp
import numpy as np
from jax import lax
from jax.experimental import pallas as pl
from jax.experimental.pallas import tpu as pltpu

HEAD_DIM = 128
N_KV = 4
N_IDX_HEADS = 16
D_IDX = 128
TOPK_MAX = 256
MAX_DISTANCE = 128
LRU_C = 8.0
EPS = 1e-6
NEG = -1e30

LANES = 128
SUBLANES = 8
BF16_ROWS = 16
VMEM_LIMIT_BYTES = 56 * 1024 * 1024
ROWWISE_TILE = 128

F32 = jnp.float32
BF16 = jnp.bfloat16
I32 = jnp.int32
INT_MIN = -(2 ** 31)
NT_DIMS = (((1,), (1,)), ((), ()))


def _sortable_key_of(x):
    b = int(np.array(x, np.float32).view(np.int32))
    return b ^ ((b >> 31) & 0x7FFFFFFF)


NEG_KEY = _sortable_key_of(NEG)


def _params(*semantics):
    return pltpu.CompilerParams(dimension_semantics=semantics, vmem_limit_bytes=VMEM_LIMIT_BYTES)


def _round_up(x, m):
    return -(-x // m) * m


def _row_tile(m, cap):
    t = min(m, cap)
    while m % t:
        t //= 2
    return t


def _sortable_keys(x):
    bits = pltpu.bitcast(x, I32)
    return bits ^ ((bits >> 31) & 0x7FFFFFFF)


def _rms(x, g):
    return x * lax.rsqrt(jnp.mean(x * x, axis=-1, keepdims=True) + EPS) * g


def _mod_kernel(c_ref, w_ref, b_ref, o_ref, a_s):
    @pl.when(pl.program_id(0) == 0)
    def _():
        c = c_ref[...]
        a_s[...] = (c * jax.nn.sigmoid(c)).astype(BF16)

    acc = jnp.dot(a_s[...], w_ref[...].astype(BF16), preferred_element_type=F32)
    o_ref[...] = acc + b_ref[...]


def modulation(c, w_mod, b_mod):
    b, d = c.shape
    n = w_mod.shape[1]
    tn = 512
    return pl.pallas_call(
        _mod_kernel,
        grid=(n // tn,),
        in_specs=[pl.BlockSpec((b, d), lambda j: (0, 0)),
                  pl.BlockSpec((d, tn), lambda j: (0, j)),
                  pl.BlockSpec((1, tn), lambda j: (0, j))],
        out_specs=pl.BlockSpec((b, tn), lambda j: (0, j)),
        out_shape=jax.ShapeDtypeStruct((b, n), F32),
        scratch_shapes=[pltpu.VMEM((b, d), BF16)],
        compiler_params=_params("arbitrary"),
        name="modulation",
    )(c, w_mod, b_mod.reshape(1, n))


def _mod_spec(mod, tm, d):
    if mod.shape[0] == 1:
        return pl.BlockSpec((1, d), lambda i: (0, 0))
    return pl.BlockSpec((tm, d), lambda i: (i, 0))


def _prenorm_kernel(x_ref, g_ref, sc_ref, sh_ref, o_ref):
    y = _rms(x_ref[...], g_ref[...])
    o_ref[...] = (y * (1.0 + sc_ref[...]) + sh_ref[...]).astype(o_ref.dtype)


def prenorm(x, g, scale, shift):
    m, d = x.shape
    tm = _row_tile(m, ROWWISE_TILE * (4 if scale.shape[0] == 1 else 1))
    return pl.pallas_call(
        _prenorm_kernel,
        grid=(m // tm,),
        in_specs=[pl.BlockSpec((tm, d), lambda i: (i, 0)),
                  pl.BlockSpec((1, d), lambda i: (0, 0)),
                  _mod_spec(scale, tm, d), _mod_spec(shift, tm, d)],
        out_specs=pl.BlockSpec((tm, d), lambda i: (i, 0)),
        out_shape=jax.ShapeDtypeStruct((m, d), BF16),
        compiler_params=_params("parallel"),
        name="prenorm",
    )(x, g.reshape(1, d), scale, shift)


def _group_norm_kernel(a_ref, b_ref, ga_ref, gb_ref, o_ref):
    da = a_ref.shape[1]
    o_ref[:, :da] = _rms(a_ref[...], ga_ref[...]).astype(o_ref.dtype)
    o_ref[:, da:] = _rms(b_ref[...], gb_ref[...]).astype(o_ref.dtype)


def group_norm_concat(a, b, ga, gb):
    m, da = a.shape
    db = b.shape[1]
    tm = _row_tile(m, ROWWISE_TILE * 4)
    return pl.pallas_call(
        _group_norm_kernel,
        grid=(m // tm,),
        in_specs=[pl.BlockSpec((tm, da), lambda i: (i, 0)),
                  pl.BlockSpec((tm, db), lambda i: (i, 0)),
                  pl.BlockSpec((1, da), lambda i: (0, 0)),
                  pl.BlockSpec((1, db), lambda i: (0, 0))],
        out_specs=pl.BlockSpec((tm, da + db), lambda i: (i, 0)),
        out_shape=jax.ShapeDtypeStruct((m, da + db), BF16),
        compiler_params=_params("parallel"),
        name="group_norm_concat",
    )(a, b, ga.reshape(1, da), gb.reshape(1, db))


def _mid_kernel(x_ref, mix_ref, gpost_ref, gate_ref, gpre_ref, sc_ref, sh_ref, x1_ref, h_ref):
    x1 = x_ref[...] + gate_ref[...] * _rms(mix_ref[...], gpost_ref[...])
    x1_ref[...] = x1
    h_ref[...] = (_rms(x1, gpre_ref[...]) * (1.0 + sc_ref[...]) + sh_ref[...]).astype(h_ref.dtype)


def mid_block(x, mixed, g_post, gate1, g_pre, scale2, shift2):
    m, d = x.shape
    tm = _row_tile(m, ROWWISE_TILE)
    row = pl.BlockSpec((tm, d), lambda i: (i, 0))
    vec = pl.BlockSpec((1, d), lambda i: (0, 0))
    return pl.pallas_call(
        _mid_kernel,
        grid=(m // tm,),
        in_specs=[row, row, vec, _mod_spec(gate1, tm, d), vec,
                  _mod_spec(scale2, tm, d), _mod_spec(shift2, tm, d)],
        out_specs=[row, row],
        out_shape=[jax.ShapeDtypeStruct((m, d), F32), jax.ShapeDtypeStruct((m, d), BF16)],
        compiler_params=_params("parallel"),
        name="mid_block",
    )(x, mixed, g_post.reshape(1, d), gate1, g_pre.reshape(1, d), scale2, shift2)


def _final_kernel(x_ref, f_ref, g_ref, gate_ref, o_ref):
    o_ref[...] = x_ref[...] + gate_ref[...] * _rms(f_ref[...], g_ref[...])


def final_block(x1, f, g_post, gate2):
    m, d = x1.shape
    tm = _row_tile(m, ROWWISE_TILE * 2)
    row = pl.BlockSpec((tm, d), lambda i: (i, 0))
    return pl.pallas_call(
        _final_kernel,
        grid=(m // tm,),
        in_specs=[row, row, pl.BlockSpec((1, d), lambda i: (0, 0)), _mod_spec(gate2, tm, d)],
        out_specs=row,
        out_shape=jax.ShapeDtypeStruct((m, d), F32),
        compiler_params=_params("parallel"),
        name="final_block",
    )(x1, f, g_post.reshape(1, d), gate2)


def _mm_w32_kernel(a_ref, w_ref, o_ref, wbf_s, *, scale, w_is_transposed):
    @pl.when(pl.program_id(1) == 0)
    def _():
        wbf_s[...] = w_ref[...].astype(BF16)

    if w_is_transposed:
        acc = lax.dot_general(a_ref[...], wbf_s[...], NT_DIMS, preferred_element_type=F32)
    else:
        acc = jnp.dot(a_ref[...], wbf_s[...], preferred_element_type=F32)
    if scale != 1.0:
        acc = acc * scale
    o_ref[...] = acc.astype(o_ref.dtype)


def matmul_w32(a, w, col0, n, out_dtype, *, scale=1.0, w_is_transposed=False, name="matmul_w32"):
    m, k = a.shape
    tm = _row_tile(m, 1024)
    tn = _row_tile(n, 512)
    assert col0 % tn == 0, (col0, tn)
    j0 = col0 // tn
    if w_is_transposed:
        w_block = (tn, k)
        w_spec = pl.BlockSpec(w_block, lambda j, i: (j0 + j, 0))
    else:
        w_block = (k, tn)
        w_spec = pl.BlockSpec(w_block, lambda j, i: (0, j0 + j))
    return pl.pallas_call(
        functools.partial(_mm_w32_kernel, scale=scale, w_is_transposed=w_is_transposed),
        grid=(n // tn, m // tm),
        in_specs=[pl.BlockSpec((tm, k), lambda j, i: (i, 0)), w_spec],
        out_specs=pl.BlockSpec((tm, tn), lambda j, i: (i, j)),
        out_shape=jax.ShapeDtypeStruct((m, n), out_dtype),
        scratch_shapes=[pltpu.VMEM(w_block, BF16)],
        compiler_params=_params("parallel", "arbitrary"),
        name=name,
    )(a, w)


def _mm_acc_kernel(a_ref, w_ref, o_ref, acc_s):
    kk = pl.program_id(2)

    @pl.when(kk == 0)
    def _():
        acc_s[...] = jnp.zeros_like(acc_s)

    acc_s[...] += jnp.dot(a_ref[...], w_ref[...], preferred_element_type=F32)

    @pl.when(kk == pl.num_programs(2) - 1)
    def _():
        o_ref[...] = acc_s[...]


def matmul_ksplit(a, w, *, k_steps, name="matmul_ksplit"):
    m, k = a.shape
    n = w.shape[1]
    tm = _row_tile(m, 1024)
    tn = _row_tile(n, 512)
    tk = k // k_steps
    return pl.pallas_call(
        _mm_acc_kernel,
        grid=(m // tm, n // tn, k_steps),
        in_specs=[pl.BlockSpec((tm, tk), lambda i, j, s: (i, s)),
                  pl.BlockSpec((tk, tn), lambda i, j, s: (s, j))],
        out_specs=pl.BlockSpec((tm, tn), lambda i, j, s: (i, j)),
        out_shape=jax.ShapeDtypeStruct((m, n), F32),
        scratch_shapes=[pltpu.VMEM((tm, tn), F32)],
        compiler_params=_params("parallel", "arbitrary", "arbitrary"),
        name=name,
    )(a, w)


def _swiglu_kernel(a_ref, wg_ref, wu_ref, o_ref, wg_s, wu_s):
    @pl.when(pl.program_id(1) == 0)
    def _():
        wg_s[...] = wg_ref[...].astype(BF16)
        wu_s[...] = wu_ref[...].astype(BF16)

    a = a_ref[...]
    g = jnp.dot(a, wg_s[...], preferred_element_type=F32)
    u = jnp.dot(a, wu_s[...], preferred_element_type=F32)
    o_ref[...] = (g * jax.nn.sigmoid(g) * u).astype(o_ref.dtype)


def swiglu(a, wg, wu):
    m, k = a.shape
    n = wg.shape[1]
    tm = _row_tile(m, 1024)
    tn = _row_tile(n, 256)
    wspec = pl.BlockSpec((k, tn), lambda j, i: (0, j))
    return pl.pallas_call(
        _swiglu_kernel,
        grid=(n // tn, m // tm),
        in_specs=[pl.BlockSpec((tm, k), lambda j, i: (i, 0)), wspec, wspec],
        out_specs=pl.BlockSpec((tm, tn), lambda j, i: (i, j)),
        out_shape=jax.ShapeDtypeStruct((m, n), BF16),
        scratch_shapes=[pltpu.VMEM((k, tn), BF16), pltpu.VMEM((k, tn), BF16)],
        compiler_params=_params("parallel", "arbitrary"),
        name="swiglu",
    )(a, wg, wu)


def _idxkey_kernel(a_ref, w_ref, g_ref, b_ref, ki_ref, wi_ref, *, wi_scale):
    z = jnp.dot(a_ref[...], w_ref[...], preferred_element_type=F32)
    ki = z[:, :D_IDX]
    mu = jnp.mean(ki, axis=-1, keepdims=True)
    var = jnp.mean(jnp.square(ki - mu), axis=-1, keepdims=True)
    ki_ref[...] = (ki - mu) * lax.rsqrt(var + EPS) * g_ref[...] + b_ref[...]
    wi_ref[...] = z[:, D_IDX:] * wi_scale


def indexer_key_proj(a, w, ln_g, ln_b, wi_scale):
    m, k = a.shape
    n = w.shape[1]
    tm = _row_tile(m, 512)
    return pl.pallas_call(
        functools.partial(_idxkey_kernel, wi_scale=wi_scale),
        grid=(m // tm,),
        in_specs=[pl.BlockSpec((tm, k), lambda i: (i, 0)),
                  pl.BlockSpec((k, n), lambda i: (0, 0)),
                  pl.BlockSpec((1, D_IDX), lambda i: (0, 0)),
                  pl.BlockSpec((1, D_IDX), lambda i: (0, 0))],
        out_specs=[pl.BlockSpec((tm, D_IDX), lambda i: (i, 0)),
                   pl.BlockSpec((tm, n - D_IDX), lambda i: (i, 0))],
        out_shape=[jax.ShapeDtypeStruct((m, D_IDX), F32),
                   jax.ShapeDtypeStruct((m, n - D_IDX), F32)],
        compiler_params=_params("parallel"),
        name="indexer_key_proj",
    )(a, w, ln_g.reshape(1, D_IDX), ln_b.reshape(1, D_IDX))


def _rg_gates(conv, wa_ref, ba_ref, wi_ref, bi_ref, lam_ref):
    cb = conv.astype(BF16)
    r = jax.nn.sigmoid(jnp.dot(cb, wa_ref[0], preferred_element_type=F32) + ba_ref[...])
    i = jax.nn.sigmoid(jnp.dot(cb, wi_ref[0], preferred_element_type=F32) + bi_ref[...])
    nl = -lam_ref[...]
    softplus = jnp.maximum(nl, 0.0) + jnp.log1p(jnp.exp(-jnp.abs(nl)))
    log_a = -LRU_C * r * softplus
    a = jnp.exp(log_a)
    u = jnp.sqrt(-jnp.tanh(log_a) * (a * a + 1.0)) * (i * conv)
    return a, u


def _rglru_prompt_kernel(xr_ref, gt_ref, cw_ref, cb_ref, wa_ref, ba_ref, wi_ref, bi_ref, lam_ref,
                         y_ref, hlast_ref, xbuf, hc):
    tt = xr_ref.shape[0]
    hist = SUBLANES

    @pl.when(pl.program_id(1) == 0)
    def _():
        xbuf[0:hist, :] = jnp.zeros((hist, LANES), F32)
        hc[...] = jnp.zeros_like(hc)

    xbuf[hist:hist + tt, :] = xr_ref[...]
    width = cw_ref.shape[0]
    conv = cb_ref[...] + xbuf[hist - width + 1:hist - width + 1 + tt, :] * cw_ref[0:1, :]
    for j in range(1, width):
        s = hist - width + 1 + j
        conv = conv + xbuf[s:s + tt, :] * cw_ref[j:j + 1, :]
    xbuf[0:hist, :] = xbuf[tt:tt + hist, :]

    a, u = _rg_gates(conv, wa_ref, ba_ref, wi_ref, bi_ref, lam_ref)

    row = lax.broadcasted_iota(I32, (tt, LANES), 0)
    d = 1
    while d < tt:
        keep = row >= d
        a_sh = jnp.where(keep, pltpu.roll(a, d, 0), 1.0)
        u_sh = jnp.where(keep, pltpu.roll(u, d, 0), 0.0)
        u = a * u_sh + u
        a = a * a_sh
        d *= 2
    h = u + a * hc[...]
    hc[...] = h[tt - 1:tt, :]
    hlast_ref[...] = h[tt - 1:tt, :]
    y_ref[...] = h * jax.nn.gelu(gt_ref[...])


def rglru_prompt(z_rg, conv_w, conv_b, w_a, b_a, w_i, b_i, lam):
    t, c2 = z_rg.shape
    c = c2 // 2
    nb = c // LANES
    tt = _row_tile(t, 1024)
    width = conv_w.shape[0]
    vec = pl.BlockSpec((1, LANES), lambda n, i: (0, n))
    wsp = pl.BlockSpec((1, LANES, LANES), lambda n, i: (n, 0, 0))
    return pl.pallas_call(
        _rglru_prompt_kernel,
        grid=(nb, t // tt),
        in_specs=[pl.BlockSpec((tt, LANES), lambda n, i: (i, n)),
                  pl.BlockSpec((tt, LANES), lambda n, i: (i, nb + n)),
                  pl.BlockSpec((width, LANES), lambda n, i: (0, n)),
                  vec, wsp, vec, wsp, vec, vec],
        out_specs=[pl.BlockSpec((tt, LANES), lambda n, i: (i, n)), vec],
        out_shape=[jax.ShapeDtypeStruct((t, c), F32), jax.ShapeDtypeStruct((1, c), F32)],
        scratch_shapes=[pltpu.VMEM((tt + SUBLANES, LANES), F32), pltpu.VMEM((1, LANES), F32)],
        compiler_params=_params("parallel", "arbitrary"),
        name="rglru_prompt",
    )(z_rg, z_rg, conv_w, conv_b.reshape(1, c), w_a.astype(BF16), b_a.reshape(1, c),
      w_i.astype(BF16), b_i.reshape(1, c), lam.reshape(1, c))


def _rglru_sample_kernel(xr_ref, gt_ref, buf_ref, h0_ref, cw_ref, cb_ref, wa_ref, ba_ref, wi_ref,
                         bi_ref, lam_ref, y_ref, hlast_ref, hist, hc):
    width = cw_ref.shape[0]

    @pl.when(pl.program_id(1) == 0)
    def _():
        hist[...] = buf_ref[...]
        hc[...] = h0_ref[...]

    x = xr_ref[0]
    conv = cb_ref[...] + hist[0] * cw_ref[0:1, :]
    for j in range(1, width - 1):
        conv = conv + hist[j] * cw_ref[j:j + 1, :]
    conv = conv + x * cw_ref[width - 1:width, :]
    for j in range(width - 2):
        hist[j] = hist[j + 1]
    hist[width - 2] = x

    a, u = _rg_gates(conv, wa_ref, ba_ref, wi_ref, bi_ref, lam_ref)
    h = a * hc[...] + u
    hc[...] = h
    hlast_ref[...] = h
    y_ref[0] = h * jax.nn.gelu(gt_ref[0])


def rglru_sample(z_rg, buf, h0, conv_w, conv_b, w_a, b_a, w_i, b_i, lam):
    t, b, c2 = z_rg.shape
    c = c2 // 2
    nb = c // LANES
    width = conv_w.shape[0]
    vec = pl.BlockSpec((1, LANES), lambda n, i: (0, n))
    wsp = pl.BlockSpec((1, LANES, LANES), lambda n, i: (n, 0, 0))
    st = pl.BlockSpec((b, LANES), lambda n, i: (0, n))
    return pl.pallas_call(
        _rglru_sample_kernel,
        grid=(nb, t),
        in_specs=[pl.BlockSpec((1, b, LANES), lambda n, i: (i, 0, n)),
                  pl.BlockSpec((1, b, LANES), lambda n, i: (i, 0, nb + n)),
                  pl.BlockSpec((width - 1, b, LANES), lambda n, i: (0, 0, n)),
                  st,
                  pl.BlockSpec((width, LANES), lambda n, i: (0, n)),
                  vec, wsp, vec, wsp, vec, vec],
        out_specs=[pl.BlockSpec((1, b, LANES), lambda n, i: (i, 0, n)), st],
        out_shape=[jax.ShapeDtypeStruct((t, b, c), F32), jax.ShapeDtypeStruct((b, c), F32)],
        scratch_shapes=[pltpu.VMEM((width - 1, b, LANES), F32), pltpu.VMEM((b, LANES), F32)],
        compiler_params=_params("parallel", "arbitrary"),
        name="rglru_sample",
    )(z_rg, z_rg, buf, h0, conv_w, conv_b.reshape(1, c), w_a.astype(BF16), b_a.reshape(1, c),
      w_i.astype(BF16), b_i.reshape(1, c), lam.reshape(1, c))


def _t5_bucket_np(dist, n_buckets):
    max_exact = n_buckets // 2
    d = np.maximum(dist, 1).astype(np.float32)
    large = max_exact + (np.log(d / np.float32(max_exact)) / np.float32(math.log(MAX_DISTANCE / max_exact))
                         * np.float32(n_buckets - max_exact)).astype(np.int32)
    large = np.minimum(large, n_buckets - 1)
    return np.where(dist < max_exact, dist, large).astype(np.int32)


def _bias_kernel(rb_ref, bucket_ref, o_ref, *, n_buckets, far_bucket, scale):
    h = pl.program_id(0)
    bucket = bucket_ref[...]
    shift = rb_ref[far_bucket, h] if far_bucket is not None else 0.0
    acc = jnp.zeros(bucket.shape, F32)
    for b in range(n_buckets):
        acc = jnp.where(bucket == b, (rb_ref[b, h] - shift) * scale, acc)
    o_ref[0] = acc


def bias_tiles(rel_bias, bucket, far_bucket, scale):
    n_buckets, n_heads = rel_bias.shape
    r, c = bucket.shape
    return pl.pallas_call(
        functools.partial(_bias_kernel, n_buckets=n_buckets, far_bucket=far_bucket, scale=scale),
        grid=(n_heads,),
        in_specs=[pl.BlockSpec(memory_space=pltpu.SMEM),
                  pl.BlockSpec((r, c), lambda h: (0, 0))],
        out_specs=pl.BlockSpec((1, r, c), lambda h: (h, 0, 0)),
        out_shape=jax.ShapeDtypeStruct((n_heads, r, c), F32),
        compiler_params=_params("parallel"),
        name="bias_tiles",
    )(rel_bias, jnp.asarray(bucket))


TQ = 128
TK = 256
TK_FULL = 512
LOG2E = math.log2(math.e)
VT_ROWS = HEAD_DIM + BF16_ROWS
NEAR_TILE_OFFSETS = (-TQ, 0, -TK)


def _pattn_kernel(qi_ref, wit_ref, q_ref, kidx_ref, k_ref, vt_ref, bias_ref, o_ref, tied_ref,
                  keys_s, qis_s, qs_s, s0_s, s1_s, acc_s, m_s, *, seq_len, n_sel, gqa, resolve_ties):
    i = pl.program_id(0)
    pos_bits = seq_len.bit_length()
    n_idx = qis_s.shape[0] // TQ

    for h in range(n_idx):
        qis_s[h * TQ:(h + 1) * TQ, :] = qi_ref[:, h * D_IDX:(h + 1) * D_IDX]
    for kv in range(N_KV):
        for g in range(gqa):
            hh = kv * gqa + g
            qs_s[kv, g * TQ:(g + 1) * TQ, :] = q_ref[:, hh * HEAD_DIM:(hh + 1) * HEAD_DIM]

    qpos = i * TQ + lax.broadcasted_iota(I32, (1, TQ), 1)

    fulls_per_tile = TK_FULL // TQ
    n_full = i // fulls_per_tile
    n_part = (i % fulls_per_tile) // (TK // TQ) + 1
    part0 = n_full * TK_FULL

    def score_tile(k0, tk, masked):
        kt = kidx_ref[pl.ds(k0, tk), :]
        acc = jnp.zeros((tk, TQ), F32)
        for hp in range(n_idx // 2):
            s = lax.dot_general(kt, qis_s[hp * 2 * TQ:(hp + 1) * 2 * TQ, :], NT_DIMS,
                                preferred_element_type=F32)
            for e in range(2):
                h = 2 * hp + e
                acc = acc + jnp.maximum(s[:, e * TQ:(e + 1) * TQ], 0.0) * wit_ref[h:h + 1, :]
        if masked:
            kpos = k0 + lax.broadcasted_iota(I32, (tk, 1), 0)
            acc = jnp.where(kpos <= qpos, acc, NEG)
        keys_s[pl.ds(k0, tk), :] = _sortable_keys(acc)

    def full_scores(j, c):
        score_tile(pl.multiple_of(j * TK_FULL, TK_FULL), TK_FULL, False)
        return c

    def part_scores(j, c):
        score_tile(pl.multiple_of(part0 + j * TK, TK), TK, True)
        return c

    lax.fori_loop(0, n_full, full_scores, 0)
    lax.fori_loop(0, n_part, part_scores, 0)

    n_after = seq_len - (part0 + n_part * TK)

    def count_ge(k0, rows, cand, cnt):
        slab = keys_s[pl.ds(k0, rows), :]
        for c in range(rows // TQ):
            u = slab[c * TQ:(c + 1) * TQ, :]
            ge = u.reshape(TQ // SUBLANES, SUBLANES, TQ) >= cand[None]
            cnt = cnt + jnp.sum(jnp.where(ge, 1, 0), axis=0)
        return cnt

    def over_tiles(tile_fn, carry):
        carry = lax.fori_loop(
            0, n_full, lambda j, c: tile_fn(pl.multiple_of(j * TK_FULL, TK_FULL), TK_FULL, c), carry)
        return lax.fori_loop(
            0, n_part, lambda j, c: tile_fn(pl.multiple_of(part0 + j * TK, TK), TK, c), carry)

    zero_cnt = jnp.zeros((SUBLANES, TQ), I32)

    def bit_step(b, carry):
        thr, n_ge = carry
        cand = thr ^ lax.shift_left(jnp.int32(1), 31 - b)
        cnt = over_tiles(lambda k0, rows, cnt: count_ge(k0, rows, cand, cnt), zero_cnt)
        tot = jnp.sum(cnt, axis=0, keepdims=True)
        tot = tot + jnp.where(cand[0:1] <= NEG_KEY, n_after, 0)
        take = tot >= n_sel
        return jnp.where(take, cand, thr), jnp.where(take, tot, n_ge)

    thr, n_ge = lax.fori_loop(
        0, 32, bit_step,
        (jnp.full((SUBLANES, TQ), INT_MIN, I32), jnp.full((SUBLANES, TQ), seq_len, I32)))
    thr = jnp.maximum(thr[0:1], INT_MIN + 1)

    tied = (n_ge[0:1] > n_sel) & (thr > NEG_KEY)
    tied_ref[...] = jnp.where(tied, 1, 0)
    last_tie = None
    if resolve_ties:
        def count_where(pred, k0, rows, cnt):
            slab = keys_s[pl.ds(k0, rows), :]
            for c in range(rows // TQ):
                u = slab[c * TQ:(c + 1) * TQ, :]
                kpos = k0 + c * TQ + lax.broadcasted_iota(I32, (TQ, 1), 0)
                hit = pred(u, kpos)
                cnt = cnt + jnp.sum(hit.reshape(TQ // SUBLANES, SUBLANES, TQ), axis=0)
            return cnt

        def total(pred):
            cnt = over_tiles(lambda k0, rows, cnt: count_where(pred, k0, rows, cnt), zero_cnt)
            return jnp.sum(cnt, axis=0, keepdims=True)

        n_tie = n_sel - total(lambda u, kpos: jnp.where(u > thr, 1, 0))

        def pos_step(b, c):
            cand = c | lax.shift_left(jnp.int32(1), pos_bits - 1 - b)
            before = total(lambda u, kpos: jnp.where(u == thr, jnp.where(kpos < cand, 1, 0), 0))
            return jnp.where(before < n_tie, cand, c)

        last_tie = lax.fori_loop(0, pos_bits, pos_step, jnp.zeros((1, TQ), I32))
        last_tie = jnp.where(tied, last_tie, seq_len)

    m_s[...] = jnp.full(m_s.shape, NEG, F32)
    acc_s[...] = jnp.zeros(acc_s.shape, F32)

    def tile_start(j):
        return j * TK if isinstance(j, int) else pl.multiple_of(j * TK, TK)

    def tile_mask(k0, near):
        u = keys_s[pl.ds(k0, TK), :]
        kpos = k0 + lax.broadcasted_iota(I32, (TK, 1), 0)
        if near:
            u = jnp.where(kpos <= qpos, u, INT_MIN)
        thr_at = thr if last_tie is None else jnp.where(kpos <= last_tie, thr, thr + 1)
        mask = jnp.where(u >= thr_at, 0.0, -jnp.inf)
        return jnp.concatenate([mask] * gqa, axis=1)

    def logits_kv(k0, s_ref, kv, mask, near_kind):
        kt = k_ref[pl.ds(k0, TK), kv * HEAD_DIM:(kv + 1) * HEAD_DIM]
        s = lax.dot_general(kt, qs_s[kv], NT_DIMS, preferred_element_type=F32)
        if near_kind is not None:
            s = s + bias_ref[near_kind, kv]
        s_ref[kv] = s + mask

    def softmax_kv(k0, s_ref, kv):
        m_old = m_s[kv]
        m_new = jnp.maximum(m_old, jnp.max(s_ref[kv], axis=0, keepdims=True))
        alpha = jnp.exp2(m_old - m_new)
        p = jnp.exp2(s_ref[kv] - m_new)
        vt = vt_ref[kv * VT_ROWS:(kv + 1) * VT_ROWS, pl.ds(k0, TK)]
        pv = jnp.dot(vt, p.astype(BF16), preferred_element_type=F32)
        acc_s[kv] = acc_s[kv] * alpha + pv
        m_s[kv] = m_new

    def logits_stage(j, s_ref, near_kind):
        k0 = tile_start(j)
        mask = tile_mask(k0, near_kind is not None)
        for kv in range(N_KV):
            logits_kv(k0, s_ref, kv, mask, near_kind)

    def softmax_stage(j, s_ref):
        k0 = tile_start(j)
        for kv in range(N_KV):
            softmax_kv(k0, s_ref, kv)

    def far_step(j_cur, s_cur, j_next, s_next):
        k0_cur, k0_next = tile_start(j_cur), tile_start(j_next)
        mask = tile_mask(k0_next, False)
        for kv in range(N_KV):
            logits_kv(k0_next, s_next, kv, mask, None)
            softmax_kv(k0_cur, s_cur, kv)

    n_tiles = (i + 2) // 2
    n_far = jnp.maximum(i - 1, 0) // 2
    last_far = jnp.maximum(n_far - 1, 0)
    logits_stage(0, s0_s, None)

    def far_quad(p, c):
        j = 4 * p
        far_step(j, s0_s, j + 1, s1_s)
        far_step(j + 1, s1_s, j + 2, s0_s)
        far_step(j + 2, s0_s, j + 3, s1_s)
        far_step(j + 3, s1_s, jnp.minimum(j + 4, last_far), s0_s)
        return c

    lax.fori_loop(0, n_far // 4, far_quad, 0)

    j_rest = (n_far // 4) * 4
    n_rest = n_far - j_rest

    @pl.when(n_rest >= 2)
    def _():
        far_step(j_rest, s0_s, j_rest + 1, s1_s)

    @pl.when(n_rest == 3)
    def _():
        far_step(j_rest + 1, s1_s, j_rest + 2, s0_s)

    @pl.when((n_rest == 1) | (n_rest == 3))
    def _():
        softmax_stage(n_far - 1, s0_s)

    @pl.when(n_rest == 2)
    def _():
        softmax_stage(n_far - 1, s1_s)

    def near_tile(j, c):
        kind = jnp.where(i % 2 == 1, 0, n_tiles - j)
        logits_stage(j, s0_s, kind)
        softmax_stage(j, s0_s)
        return c

    lax.fori_loop(n_far, n_tiles, near_tile, 0)

    for kv in range(N_KV):
        ot = acc_s[kv, :HEAD_DIM, :] * (1.0 / acc_s[kv, HEAD_DIM:HEAD_DIM + 1, :])
        for g in range(gqa):
            hh = kv * gqa + g
            o_ref[:, hh * HEAD_DIM:(hh + 1) * HEAD_DIM] = ot[:, g * TQ:(g + 1) * TQ].T


def prompt_attention(q, qi, wi_t, kidx, k, v_t, bias, n_sel):
    t, dq = q.shape
    n_heads = dq // HEAD_DIM
    gqa = n_heads // N_KV
    n_idx = qi.shape[1] // D_IDX
    dkv = N_KV * HEAD_DIM
    whole = lambda shape: pl.BlockSpec(shape, lambda i: (0,) * len(shape))

    def run(resolve_ties):
        return pl.pallas_call(
            functools.partial(_pattn_kernel, seq_len=t, n_sel=n_sel, gqa=gqa, resolve_ties=resolve_ties),
            grid=(t // TQ,),
            in_specs=[pl.BlockSpec((TQ, n_idx * D_IDX), lambda i: (i, 0)),
                      pl.BlockSpec((n_idx, TQ), lambda i: (0, i)),
                      pl.BlockSpec((TQ, dq), lambda i: (i, 0)),
                      whole((t, D_IDX)), whole((t, dkv)), whole((N_KV * VT_ROWS, t)),
                      whole(bias.shape)],
            out_specs=[pl.BlockSpec((TQ, dq), lambda i: (i, 0)),
                       pl.BlockSpec((1, TQ), lambda i: (0, i))],
            out_shape=[jax.ShapeDtypeStruct((t, dq), F32), jax.ShapeDtypeStruct((1, t), I32)],
            scratch_shapes=[pltpu.VMEM((t, TQ), I32),
                            pltpu.VMEM((n_idx * TQ, D_IDX), BF16),
                            pltpu.VMEM((N_KV, gqa * TQ, HEAD_DIM), BF16),
                            pltpu.VMEM((N_KV, TK, gqa * TQ), F32),
                            pltpu.VMEM((N_KV, TK, gqa * TQ), F32),
                            pltpu.VMEM((N_KV, VT_ROWS, gqa * TQ), F32),
                            pltpu.VMEM((N_KV, 1, gqa * TQ), F32)],
            compiler_params=_params("parallel"),
            name="prompt_attention_ties" if resolve_ties else "prompt_attention",
        )(qi, wi_t, q, kidx, k, v_t, bias)

    out, tied = run(False)
    return lax.cond(jnp.any(tied > 0), lambda: run(True)[0], lambda: out)


RADIX_BITS = 4

def _sattn_kernel(pt_ref, qi_ref, wi_ref, q_ref, kin_ref, kn_ref, vn_ref, bias_ref, *rest,
                  n_pages, page, n_sel, gqa, resolve_ties):
    ki_pages = rest[:n_pages]
    k_pages = rest[n_pages:2 * n_pages]
    v_pages = rest[2 * n_pages:3 * n_pages]
    o_ref, tied_ref, kibuf, kbuf, vbuf = rest[3 * n_pages:]
    del pt_ref
    past = n_pages * page
    lp = kibuf.shape[0]
    t_new = o_ref.shape[1]
    n_new = kin_ref.shape[1]
    rows = q_ref.shape[1]
    dkv = N_KV * HEAD_DIM

    for p in range(n_pages):
        kibuf[p * page:(p + 1) * page, :] = ki_pages[p][0].astype(BF16)
        for kv in range(N_KV):
            rows_kv = pl.ds(kv, page, stride=N_KV)
            cols_kv = slice(kv * HEAD_DIM, (kv + 1) * HEAD_DIM)
            kbuf[p * page:(p + 1) * page, cols_kv] = k_pages[p][0, rows_kv, :].astype(BF16)
            vbuf[p * page:(p + 1) * page, cols_kv] = v_pages[p][0, rows_kv, :].astype(BF16)
    kibuf[past:, :] = jnp.zeros((lp - past, D_IDX), BF16)
    kbuf[past:, :] = jnp.zeros((lp - past, dkv), BF16)
    vbuf[past:, :] = jnp.zeros((lp - past, dkv), BF16)
    kibuf[past:past + n_new, :] = kin_ref[0]
    kbuf[past:past + n_new, :] = kn_ref[0]
    vbuf[past:past + n_new, :] = vn_ref[0]

    kpos = lax.broadcasted_iota(I32, (1, lp), 1)
    qpos = past + lax.broadcasted_iota(I32, (t_new, 1), 0)
    visible = kpos <= qpos

    s = lax.dot_general(qi_ref[0], kibuf[...], NT_DIMS, preferred_element_type=F32)
    s = jnp.maximum(s, 0.0) * wi_ref[0]
    n_idx = s.shape[0] // t_new
    score = jnp.sum(s.reshape(n_idx, t_new, lp), axis=0)
    score = jnp.where(visible, score, NEG)
    keys = jnp.where(kpos < past + t_new, _sortable_keys(score), INT_MIN)

    prefix = jnp.zeros((t_new, 1), I32)
    for shift in range(32 - RADIX_BITS, -1, -RADIX_BITS):
        digit = jnp.zeros((t_new, 1), I32)
        for v in range(1, 2 ** RADIX_BITS):
            step = int(np.array((v << shift) & 0xFFFFFFFF, np.uint32).view(np.int32))
            cand = (prefix | step) ^ INT_MIN
            tot = jnp.sum(jnp.where(keys >= cand, 1, 0), axis=1, keepdims=True)
            digit = digit + jnp.where(tot >= n_sel, 1, 0)
        prefix = prefix | lax.shift_left(digit, jnp.int32(shift))
    thr = jnp.maximum(prefix ^ INT_MIN, INT_MIN + 1)

    n_ge = jnp.sum(jnp.where(keys >= thr, 1, 0), axis=1, keepdims=True)
    tied = (n_ge > n_sel) & (thr > NEG_KEY)
    tied_ref[0] = jnp.where(tied, 1, 0)
    if resolve_ties:
        pos_bits = lp.bit_length()
        n_tie = n_sel - jnp.sum(jnp.where(keys > thr, 1, 0), axis=1, keepdims=True)

        def pos_step(b, c):
            cand = c | lax.shift_left(jnp.int32(1), pos_bits - 1 - b)
            before = jnp.sum(jnp.where(keys == thr, jnp.where(kpos < cand, 1, 0), 0), axis=1, keepdims=True)
            return jnp.where(before < n_tie, cand, c)

        last_tie = lax.fori_loop(0, pos_bits, pos_step, jnp.zeros((t_new, 1), I32))
        thr = jnp.where(kpos <= jnp.where(tied, last_tie, lp), thr, thr + 1)
    sel = jnp.where(visible, keys, INT_MIN) >= thr

    qrep = jnp.concatenate([q_ref[0]] * N_KV, axis=1)
    rkv = lax.broadcasted_iota(I32, (rows, dkv), 0) // (gqa * t_new)
    ckv = lax.broadcasted_iota(I32, (rows, dkv), 1) // HEAD_DIM
    qblk = jnp.where(rkv == ckv, qrep, jnp.zeros_like(qrep))
    logits = lax.dot_general(qblk, kbuf[...], NT_DIMS, preferred_element_type=F32) + bias_ref[...]
    n_heads = rows // t_new
    logits = jnp.where(sel[None], logits.reshape(n_heads, t_new, lp), -jnp.inf)
    m = jnp.maximum(jnp.max(logits, axis=-1, keepdims=True), NEG)
    p = jnp.exp2(logits - m)
    l = jnp.sum(p, axis=-1, keepdims=True)
    o = jnp.dot(p.reshape(rows, lp).astype(BF16), vbuf[...], preferred_element_type=F32)
    o = o.reshape(n_heads, t_new, dkv) * (1.0 / l)
    for kv in range(N_KV):
        for g in range(gqa):
            hh = kv * gqa + g
            o_ref[0, :, hh * HEAD_DIM:(hh + 1) * HEAD_DIM] = o[hh, :, kv * HEAD_DIM:(kv + 1) * HEAD_DIM]


def sample_attention(page_table, qi, wi, q, ki_new, k_new, v_new, bias, cache_ki, cache_k, cache_v,
                     t_new, n_sel):
    b, rows, _ = q.shape
    n_heads = rows // t_new
    gqa = n_heads // N_KV
    n_pages = page_table.shape[1]
    page = cache_ki.shape[1]
    dkv = N_KV * HEAD_DIM
    lp = bias.shape[1]
    n_new = ki_new.shape[1]
    per_seq = lambda shape: pl.BlockSpec((1,) + shape, lambda s, pt: (s, 0, 0))

    def page_spec(page_rows, p):
        return pl.BlockSpec((1, page_rows, HEAD_DIM), lambda s, pt: (pt[s, p], 0, 0))

    in_specs = [per_seq((qi.shape[1], D_IDX)), per_seq((wi.shape[1], 1)), per_seq((rows, HEAD_DIM)),
                per_seq((n_new, D_IDX)), per_seq((n_new, dkv)), per_seq((n_new, dkv)),
                pl.BlockSpec((rows, lp), lambda s, pt: (0, 0))]
    in_specs += [page_spec(page, p) for p in range(n_pages)]
    in_specs += [page_spec(page * N_KV, p) for p in range(n_pages)]
    in_specs += [page_spec(page * N_KV, p) for p in range(n_pages)]
    grid_spec = pltpu.PrefetchScalarGridSpec(
        num_scalar_prefetch=1,
        grid=(b,),
        in_specs=in_specs,
        out_specs=[pl.BlockSpec((1, t_new, n_heads * HEAD_DIM), lambda s, pt: (s, 0, 0)),
                   pl.BlockSpec((1, t_new, 1), lambda s, pt: (s, 0, 0))],
        scratch_shapes=[pltpu.VMEM((lp, D_IDX), BF16), pltpu.VMEM((lp, dkv), BF16),
                        pltpu.VMEM((lp, dkv), BF16)],
    )
    operands = (page_table, qi, wi, q, ki_new, k_new, v_new, bias,
                *([cache_ki] * n_pages), *([cache_k] * n_pages), *([cache_v] * n_pages))

    def run(resolve_ties):
        return pl.pallas_call(
            functools.partial(_sattn_kernel, n_pages=n_pages, page=page, n_sel=n_sel, gqa=gqa,
                              resolve_ties=resolve_ties),
            grid_spec=grid_spec,
            out_shape=[jax.ShapeDtypeStruct((b, t_new, n_heads * HEAD_DIM), F32),
                       jax.ShapeDtypeStruct((b, t_new, 1), I32)],
            compiler_params=_params("parallel"),
            name="sample_attention_ties" if resolve_ties else "sample_attention",
        )(*operands)

    out, tied = run(False)
    return lax.cond(jnp.any(tied > 0), lambda: run(True)[0], lambda: out)


def _pad_cols(w, n):
    return jnp.pad(w, ((0, 0), (0, n - w.shape[1])))


def _prompt_bias(rel_bias, gqa):
    n_buckets = rel_bias.shape[0]
    n_kinds = len(NEAR_TILE_OFFSETS)
    kl = np.arange(TK)[:, None]
    ql = np.arange(TQ)[None, :]
    dist = np.stack([np.maximum(ql - (off + kl), 0) for off in NEAR_TILE_OFFSETS])
    bucket = _t5_bucket_np(dist, n_buckets).reshape(n_kinds * TK, TQ)
    far = int(_t5_bucket_np(np.array([MAX_DISTANCE + 1]), n_buckets)[0])
    tiles = bias_tiles(rel_bias, bucket, far, LOG2E)
    tiles = tiles.reshape(N_KV, gqa, n_kinds, TK, TQ).transpose(2, 0, 3, 1, 4)
    return tiles.reshape(n_kinds, N_KV, TK, gqa * TQ)


def _sample_bias(rel_bias, past, t_new, lp):
    n_buckets, n_heads = rel_bias.shape
    dist = np.maximum(past + np.arange(t_new)[:, None] - np.arange(lp)[None, :], 0)
    tiles = bias_tiles(rel_bias, _t5_bucket_np(dist, n_buckets), None, LOG2E)
    return tiles.reshape(n_heads * t_new, lp)


def _mixer_in(x, mods, g_pre_mix, w, ln_g, ln_b, sizes):
    d_rnn, d_q, d_kv, d_qi = sizes
    h1 = prenorm(x, g_pre_mix, mods[1], mods[0])
    c_q = 2 * d_rnn
    c_kv = c_q + d_q
    c_qi = c_kv + 2 * d_kv
    proj = functools.partial(matmul_w32, h1, w["in_t"], w_is_transposed=True)
    z_rg = proj(0, c_q, F32, name="in_proj_rg")
    q = proj(c_q, d_q, BF16, scale=HEAD_DIM ** -0.5 * LOG2E, name="in_proj_q")
    kv = proj(c_kv, 2 * d_kv, F32, name="in_proj_kv")
    qi = proj(c_qi, d_qi, BF16, name="in_proj_qi")
    n_idx = d_qi // D_IDX
    ki, wi = indexer_key_proj(h1, w["kiwi"], ln_g, ln_b, float(d_qi) ** -0.5)
    return z_rg, q, kv[:, :d_kv], kv[:, d_kv:], qi, ki, wi[:, :n_idx]


def _block_out(x, y_rnn, y_attn, mods, w, g_out_rnn, g_out_attn, g_post_mix, g_pre_ffn, g_post_ffn):
    ycat = group_norm_concat(y_rnn, y_attn, g_out_rnn, g_out_attn)
    mixed = matmul_w32(ycat, w["out"], 0, w["out"].shape[1], F32, name="out_proj")
    x1, h2 = mid_block(x, mixed, g_post_mix, mods[2], g_pre_ffn, mods[4], mods[3])
    hf = swiglu(h2, w["ffn_gate"], w["ffn_up"])
    f = matmul_ksplit(hf, w["ffn_down"], k_steps=2, name="ffn_down")
    return final_block(x1, f, g_post_ffn, mods[5])


def kernel(x_prompt, x_sample, cache_k, cache_v, cache_idx_k, state_conv, state_h, page_table,
           c_prompt, c_sample, rel_bias, w_mod, b_mod, g_pre_mix, g_post_mix, g_pre_ffn, g_post_ffn,
           w_in, conv_w, conv_b, w_rg_a, b_rg_a, w_rg_i, b_rg_i, lru_lambda, idx_k_ln_g, idx_k_ln_b,
           g_out_rnn, g_out_attn, w_out, w_ffn_gate, w_ffn_up, w_ffn_down):
    depth = w_in.shape[0]
    assert depth == 1, "one layer: prompt/sample caches are not threaded through deeper stacks"
    bp, t_p, d = x_prompt.shape
    assert bp == 1, "the prompt group is one sequence"
    bs, t_s, _ = x_sample.shape
    d_rnn = w_rg_a.shape[1] * w_rg_a.shape[2]
    d_attn = d - d_rnn
    n_heads = d_attn // HEAD_DIM
    gqa = n_heads // N_KV
    d_kv = N_KV * HEAD_DIM
    d_qi = N_IDX_HEADS * D_IDX
    n_pages = page_table.shape[1]
    page = cache_k.shape[2]
    past = n_pages * page
    width = conv_w.shape[1]
    sizes = (d_rnn, d_attn, d_kv, d_qi)

    w_in0 = w_in[0]
    c3 = 2 * d_rnn + d_attn + 2 * d_kv + d_qi
    w = {
        "in_t": jnp.swapaxes(w_in0, 0, 1),
        "kiwi": _pad_cols(w_in0[:, c3:], D_IDX + LANES).astype(BF16),
        "out": w_out[0],
        "ffn_gate": w_ffn_gate[0],
        "ffn_up": w_ffn_up[0],
        "ffn_down": w_ffn_down[0].astype(BF16),
    }

    n_c = _round_up(bs + 1, BF16_ROWS)
    c_all = jnp.concatenate([c_sample, c_prompt, jnp.zeros((n_c - bs - 1, d), F32)], axis=0)
    mod_all = modulation(c_all, w_mod[0], b_mod[0])
    mods_p = [mod_all[bs:bs + 1, j * d:(j + 1) * d] for j in range(6)]
    mods_s = [jnp.repeat(mod_all[:bs, j * d:(j + 1) * d], t_s, axis=0) for j in range(6)]

    rnn_w = (conv_w[0], conv_b[0], w_rg_a[0], b_rg_a[0], w_rg_i[0], b_rg_i[0], lru_lambda[0])
    out_w = (g_out_rnn[0], g_out_attn[0], g_post_mix[0], g_pre_ffn[0], g_post_ffn[0])

    xp = x_prompt.reshape(t_p, d)
    z_rg, q, k, v, qi, ki, wi = _mixer_in(xp, mods_p, g_pre_mix[0], w, idx_k_ln_g[0], idx_k_ln_b[0], sizes)
    y_rnn, h_p = rglru_prompt(z_rg, *rnn_w)
    n_sel_p = min(TOPK_MAX, t_p // 4)
    v_heads = v.astype(BF16).reshape(t_p, N_KV, HEAD_DIM)
    v_ones = jnp.ones((t_p, N_KV, VT_ROWS - HEAD_DIM), BF16)
    v_t = jnp.concatenate([v_heads, v_ones], axis=2).reshape(t_p, N_KV * VT_ROWS).T
    y_attn = prompt_attention(q, qi, wi.T, ki.astype(BF16), k.astype(BF16), v_t,
                              _prompt_bias(rel_bias, gqa), n_sel_p)
    y_p = _block_out(xp, y_rnn, y_attn, mods_p, w, *out_w)
    k_prompt = k.reshape(1, 1, t_p, N_KV, HEAD_DIM)
    v_prompt = v.reshape(1, 1, t_p, N_KV, HEAD_DIM)
    kidx_prompt = ki.reshape(1, 1, t_p, D_IDX)
    conv_prompt = z_rg[t_p - (width - 1):, :d_rnn].reshape(1, 1, width - 1, d_rnn)
    h_prompt = h_p.reshape(1, 1, d_rnn)

    m_s = bs * t_s
    xs = x_sample.reshape(m_s, d)
    z_rg, q, k, v, qi, ki, wi = _mixer_in(xs, mods_s, g_pre_mix[0], w, idx_k_ln_g[0], idx_k_ln_b[0], sizes)
    z_t = z_rg.reshape(bs, t_s, 2 * d_rnn).transpose(1, 0, 2)
    y_t, h_s = rglru_sample(z_t, state_conv[0].transpose(1, 0, 2), state_h[0], *rnn_w)
    y_rnn = y_t.transpose(1, 0, 2).reshape(m_s, d_rnn)

    def head_major(a, n):
        return a.reshape(bs, t_s, n, a.shape[1] // n).transpose(0, 2, 1, 3).reshape(bs, n * t_s, -1)

    n_new = _round_up(t_s, BF16_ROWS)

    def new_rows(a):
        a = a.reshape(bs, t_s, -1).astype(BF16)
        return jnp.pad(a, ((0, 0), (0, n_new - t_s), (0, 0)))

    lp = past + _round_up(t_s, LANES)
    n_sel_s = min(TOPK_MAX, (past + t_s) // 4)
    wi_rows = wi.reshape(bs, t_s, N_IDX_HEADS).transpose(0, 2, 1).reshape(bs, N_IDX_HEADS * t_s, 1)
    y_attn = sample_attention(
        page_table, head_major(qi, N_IDX_HEADS), wi_rows, head_major(q, n_heads),
        new_rows(ki), new_rows(k), new_rows(v), _sample_bias(rel_bias, past, t_s, lp),
        cache_idx_k[0], cache_k[0].reshape(-1, page * N_KV, HEAD_DIM),
        cache_v[0].reshape(-1, page * N_KV, HEAD_DIM),
        t_s, n_sel_s).reshape(m_s, d_attn)
    y_s = _block_out(xs, y_rnn, y_attn, mods_s, w, *out_w)
    k_sample = k.reshape(1, bs, t_s, N_KV, HEAD_DIM)
    v_sample = v.reshape(1, bs, t_s, N_KV, HEAD_DIM)
    kidx_sample = ki.reshape(1, bs, t_s, D_IDX)
    xr_s = z_rg[:, :d_rnn].reshape(bs, t_s, d_rnn)
    hist = jnp.concatenate([state_conv[0], xr_s], axis=1)[:, t_s:]
    conv_sample = hist.reshape(1, bs, width - 1, d_rnn)
    h_sample = h_s.reshape(1, bs, d_rnn)

    return (y_p.reshape(1, t_p, d), y_s.reshape(bs, t_s, d), k_prompt, v_prompt, kidx_prompt,
            conv_prompt, h_prompt, k_sample, v_sample, kidx_sample, conv_sample, h_sample)
```
